```python
import jax, jax.numpy as jnp
from jax import lax
import numpy as np

D_MODEL = 1024
BATCH = 16
SEQ = 2048
DEPTH = 2

MEM_LEN = 256
C_CONV = D_MODEL // 2
CONV_K = 31
C_POOL = D_MODEL // 2
POOL_WINDOWS = (2, 4, 8, 16)
N_POOL_GROUPS = len(POOL_WINDOWS)
POOL_GROUP_DIM = C_POOL // N_POOL_GROUPS
POOL_GROUP_OUT = D_MODEL // N_POOL_GROUPS
N_IN = 2 * C_CONV + C_POOL + 2 * D_MODEL
XA_HEADS = 4
XA_HEAD_DIM = D_MODEL // XA_HEADS
D_FF = 2816
FFN_CONV_K = 3
EPS = 1e-6

kernel_name = "hybrid_conformer_pool_gated_block"


def rms_norm(x, g):
    xf = x.astype(jnp.float32)
    y = xf * lax.rsqrt(jnp.mean(xf * xf, axis=-1, keepdims=True) + EPS)
    return (y * g.astype(jnp.float32)).astype(x.dtype)


def layer_norm(x, g, b):
    xf = x.astype(jnp.float32)
    mu = jnp.mean(xf, axis=-1, keepdims=True)
    xc = xf - mu
    var = jnp.mean(xc * xc, axis=-1, keepdims=True)
    y = xc * lax.rsqrt(var + EPS) * g.astype(jnp.float32) + b.astype(jnp.float32)
    return y.astype(x.dtype)


def causal_dwconv(u, w):
    k = w.shape[0]
    return lax.conv_general_dilated(
        u, w[:, None, :].astype(u.dtype), window_strides=(1,), padding=[(k - 1, 0)],
        dimension_numbers=("NWC", "WIO", "NWC"), feature_group_count=u.shape[-1])


def multiscale_pool(u):
    b, s, _ = u.shape
    uf = u.astype(jnp.float32).reshape(b, s, N_POOL_GROUPS, POOL_GROUP_DIM)
    cs = jnp.cumsum(uf, axis=1)
    t = jnp.arange(s)
    outs = []
    for g, w in enumerate(POOL_WINDOWS):
        c = cs[:, :, g]
        lag = jnp.pad(c, ((0, 0), (w, 0), (0, 0)))[:, :s]
        cnt = jnp.minimum(t + 1, w).astype(jnp.float32)[None, :, None]
        outs.append((c - lag) / cnt - uf[:, :, g])
    return jnp.stack(outs, axis=2).astype(u.dtype)


def setup_inputs(seed: int = 0) -> dict:
    key = jax.random.key(seed)
    ks = jax.random.split(key, 24)
    f32 = jnp.float32
    nrm = lambda k, shape, scale: jax.random.normal(k, shape, f32) * scale
    gain = lambda k, shape: 1.0 + 0.05 * jax.random.normal(k, shape, f32)
    return {
        "x": jax.random.normal(ks[0], (BATCH, SEQ, D_MODEL), f32),
        "mem": jax.random.normal(ks[1], (BATCH, MEM_LEN, D_MODEL), f32),
        "mix_norm_g": gain(ks[2], (DEPTH, D_MODEL)),
        "w_in": nrm(ks[3], (DEPTH, D_MODEL, N_IN), D_MODEL ** -0.5),
        "conv_dw_w": nrm(ks[4], (DEPTH, CONV_K, C_CONV), CONV_K ** -0.5),
        "conv_dw_b": nrm(ks[5], (DEPTH, C_CONV), 0.02),
        "conv_ln_g": gain(ks[6], (DEPTH, C_CONV)),
        "conv_ln_b": nrm(ks[7], (DEPTH, C_CONV), 0.02),
        "w_conv_out": nrm(ks[8], (DEPTH, C_CONV, D_MODEL), C_CONV ** -0.5),
        "w_pool_grp": nrm(ks[9], (DEPTH, N_POOL_GROUPS, POOL_GROUP_DIM, POOL_GROUP_OUT), POOL_GROUP_DIM ** -0.5),
        "pool_scale": gain(ks[10], (DEPTH, D_MODEL)),
        "w_out": nrm(ks[11], (DEPTH, D_MODEL, D_MODEL), D_MODEL ** -0.5),
        "xattn_norm_g": gain(ks[12], (DEPTH, D_MODEL)),
        "mem_norm_g": gain(ks[13], (D_MODEL,)),
        "w_q": nrm(ks[14], (DEPTH, D_MODEL, D_MODEL), D_MODEL ** -0.5),
        "w_kv": nrm(ks[15], (DEPTH, D_MODEL, 2 * D_MODEL), D_MODEL ** -0.5),
        "w_o": nrm(ks[16], (DEPTH, D_MODEL, D_MODEL), D_MODEL ** -0.5),
        "ffn_norm_g": gain(ks[17], (DEPTH, D_MODEL)),
        "w_up": nrm(ks[18], (DEPTH, D_MODEL, 2 * D_FF), D_MODEL ** -0.5),
        "ffn_dw_w": nrm(ks[19], (DEPTH, FFN_CONV_K, 2 * D_FF), FFN_CONV_K ** -0.5),
        "w_down": nrm(ks[20], (DEPTH, D_FF, D_MODEL), D_FF ** -0.5),
        "final_norm_g": gain(ks[21], (D_MODEL,)),
    }


def reference(x, mem, mix_norm_g, w_in, conv_dw_w, conv_dw_b, conv_ln_g, conv_ln_b, w_conv_out,
              w_pool_grp, pool_scale, w_out, xattn_norm_g, mem_norm_g, w_q, w_kv, w_o,
              ffn_norm_g, w_up, ffn_dw_w, w_down, final_norm_g):
    b, s, d = x.shape
    m_len = mem.shape[1]
    mem_n = rms_norm(mem, mem_norm_g)
    split_at = [C_CONV, 2 * C_CONV, 2 * C_CONV + C_POOL, 2 * C_CONV + C_POOL + D_MODEL]
    xa_scale = XA_HEAD_DIM ** -0.5
    for l in range(DEPTH):
        h = rms_norm(x, mix_norm_g[l])
        proj = h @ w_in[l]
        a, gl, u_pool, g_conv, g_pool = jnp.split(proj, split_at, axis=-1)
        yc = a * jax.nn.sigmoid(gl)
        yc = causal_dwconv(yc, conv_dw_w[l]) + conv_dw_b[l]
        yc = jax.nn.silu(layer_norm(yc, conv_ln_g[l], conv_ln_b[l]))
        yc = yc @ w_conv_out[l]
        zp = multiscale_pool(u_pool)
        yp = jnp.einsum("bsgc,gcd->bsgd", zp, w_pool_grp[l]).reshape(b, s, d) * pool_scale[l]
        merged = jax.nn.sigmoid(g_conv) * yc + jax.nn.sigmoid(g_pool) * yp
        x = x + merged @ w_out[l]
        hq = rms_norm(x, xattn_norm_g[l])
        q = (hq @ w_q[l]).reshape(b, s, XA_HEADS, XA_HEAD_DIM)
        kv = mem_n @ w_kv[l]
        k, v = jnp.split(kv, 2, axis=-1)
        k = k.reshape(b, m_len, XA_HEADS, XA_HEAD_DIM)
        v = v.reshape(b, m_len, XA_HEADS, XA_HEAD_DIM)
        sc = jnp.einsum("bshd,bmhd->bhsm", q, k).astype(jnp.float32) * xa_scale
        pr = jax.nn.softmax(sc, axis=-1).astype(v.dtype)
        att = jnp.einsum("bhsm,bmhd->bshd", pr, v).reshape(b, s, d)
        x = x + att @ w_o[l]
        hf = rms_norm(x, ffn_norm_g[l])
        up = causal_dwconv(hf @ w_up[l], ffn_dw_w[l])
        gate, val = jnp.split(up, 2, axis=-1)
        x = x + (jax.nn.gelu(gate) * val) @ w_down[l]
    return rms_norm(x, final_norm_g)
```

```python
import functools

import jax
import jax.numpy as jnp
from jax import lax
from jax.experimental import pallas as pl
from jax.experimental.pallas import tpu as pltpu

EPS = 1e-6
POOL_WINDOWS = (2, 4, 8, 16)
XA_HEADS = 4

F32 = jnp.float32
BF16 = jnp.bfloat16

SEQ_TILE = 512
CONV_ROWS = 32
SUBLANES = 8
VMEM_LIMIT = 52 * 1024 * 1024


def _dot(a, b):
    return jnp.dot(a, b, preferred_element_type=F32)


def _rms(x, g):
    return x * lax.rsqrt(jnp.mean(x * x, axis=-1, keepdims=True) + EPS) * g


def _const_spec(shape, layer=None):
    n = len(shape)
    if layer is None:
        return pl.BlockSpec(shape, lambda b, s: (0,) * n, pipeline_mode=pl.Buffered(1))
    return pl.BlockSpec((None,) + shape, lambda b, s: (layer,) + (0,) * n,
                        pipeline_mode=pl.Buffered(1))


def _params():
    return pltpu.CompilerParams(dimension_semantics=("arbitrary", "arbitrary"),
                                vmem_limit_bytes=VMEM_LIMIT)


def _kv_kernel(mem_ref, g_ref, wkv_ref, k_ref, v_ref):
    d = mem_ref.shape[-1]
    mn = _rms(mem_ref[...], g_ref[...]).astype(BF16)
    k_ref[...] = _dot(mn, wkv_ref[:, :d]).astype(BF16)
    v_ref[...] = _dot(mn, wkv_ref[:, d:]).astype(BF16)


def _memory_kv(mem, mem_norm_g, w_kv_bf):
    depth = w_kv_bf.shape[0]
    b, m, d = mem.shape
    out = jax.ShapeDtypeStruct((depth, b, m, d), BF16)
    return pl.pallas_call(
        _kv_kernel,
        grid=(depth, b),
        in_specs=[pl.BlockSpec((None, m, d), lambda l, i: (i, 0, 0)),
                  pl.BlockSpec((1, d), lambda l, i: (0, 0)),
                  pl.BlockSpec((None, d, 2 * d), lambda l, i: (l, 0, 0))],
        out_specs=[pl.BlockSpec((None, None, m, d), lambda l, i: (l, i, 0, 0)),
                   pl.BlockSpec((None, None, m, d), lambda l, i: (l, i, 0, 0))],
        out_shape=[out, out],
        compiler_params=_params(),
        name="memory_kv",
    )(mem, mem_norm_g.reshape(1, d), w_kv_bf)


def _mixer_kernel(x_ref, g_ref, win_ref, dww_ref, dwb_ref, lng_ref, lnb_ref, wco_ref, wpg_ref,
                  ps_ref, wout_ref, o_ref, ybuf, ubuf, cbuf, mbuf, *, hist):
    ts, d = x_ref.shape
    cc = wco_ref.shape[0]
    cp = ubuf.shape[1]
    kc = dww_ref.shape[0]
    ngrp = wpg_ref.shape[0]
    gdim = wpg_ref.shape[1]
    gout = wpg_ref.shape[2]
    s = pl.program_id(1)

    @pl.when(s == 0)
    def _():
        ybuf[0:hist, :] = jnp.zeros((hist, cc), F32)
        ubuf[0:hist, :] = jnp.zeros((hist, cp), F32)

    x = x_ref[...]
    h = _rms(x, g_ref[...]).astype(BF16)

    a = _dot(h, win_ref[:, 0:cc])
    gl = _dot(h, win_ref[:, cc:2 * cc])
    ybuf[hist:hist + ts, :] = a * jax.nn.sigmoid(gl)
    ubuf[hist:hist + ts, :] = _dot(h, win_ref[:, 2 * cc:2 * cc + cp])

    bias = dwb_ref[...]
    lng = lng_ref[...]
    lnb = lnb_ref[...]

    for i in range(ts // CONV_ROWS):
        r0 = i * CONV_ROWS
        acc = jnp.zeros((CONV_ROWS, cc), F32) + bias
        for k in range(kc):
            lo = r0 + hist - (kc - 1) + k
            acc = acc + dww_ref[k:k + 1, :] * ybuf[lo:lo + CONV_ROWS, :]
        mu = jnp.mean(acc, axis=-1, keepdims=True)
        xc = acc - mu
        var = jnp.mean(xc * xc, axis=-1, keepdims=True)
        y = xc * lax.rsqrt(var + EPS) * lng + lnb
        cbuf[r0:r0 + CONV_ROWS, :] = (y * jax.nn.sigmoid(y)).astype(BF16)

    t_glob = s * ts + lax.broadcasted_iota(jnp.int32, (ts, gdim), 0)
    g_conv0 = 2 * cc + cp
    g_pool0 = g_conv0 + d
    for j in range(ngrp):
        w = POOL_WINDOWS[j]
        lanes = slice(j * gdim, (j + 1) * gdim)
        cols = slice(j * gout, (j + 1) * gout)
        u = ubuf[hist:hist + ts, lanes]
        tot = u
        for back in range(1, w):
            tot = tot + ubuf[hist - back:hist - back + ts, lanes]
        cnt = jnp.minimum(t_glob + 1, w).astype(F32)
        zp = (tot / cnt - u).astype(BF16)
        yp = _dot(zp, wpg_ref[j]) * ps_ref[:, cols]
        yc = _dot(cbuf[...], wco_ref[:, cols])
        gc = _dot(h, win_ref[:, g_conv0 + j * gout:g_conv0 + (j + 1) * gout])
        gp = _dot(h, win_ref[:, g_pool0 + j * gout:g_pool0 + (j + 1) * gout])
        mbuf[:, cols] = (jax.nn.sigmoid(gc) * yc + jax.nn.sigmoid(gp) * yp).astype(BF16)

    o_ref[...] = x + _dot(mbuf[...], wout_ref[...])

    ybuf[0:hist, :] = ybuf[ts:ts + hist, :]
    ubuf[0:hist, :] = ubuf[ts:ts + hist, :]


def _mixer(x, layer, g, w_in, dw_w, dw_b, ln_g, ln_b, w_co, w_pg, p_scale, w_out):
    b, s, d = x.shape
    ts = min(SEQ_TILE, s)
    n_in = w_in.shape[-1]
    kc, cc = dw_w.shape[1:]
    ngrp, gdim, gout = w_pg.shape[1:]
    cp = ngrp * gdim
    hist = -(-max(kc - 1, max(POOL_WINDOWS) - 1) // CONV_ROWS) * CONV_ROWS
    assert s % ts == 0 and ts % CONV_ROWS == 0 and hist <= ts
    assert n_in == 2 * cc + cp + 2 * d and ngrp == len(POOL_WINDOWS) and ngrp * gout == d
    xspec = pl.BlockSpec((None, ts, d), lambda i, j: (i, j, 0))
    return pl.pallas_call(
        functools.partial(_mixer_kernel, hist=hist),
        grid=(b, s // ts),
        in_specs=[xspec,
                  _const_spec((1, d), layer),
                  _const_spec((d, n_in), layer),
                  _const_spec((kc, cc), layer),
                  _const_spec((1, cc), layer),
                  _const_spec((1, cc), layer),
                  _const_spec((1, cc), layer),
                  _const_spec((cc, d), layer),
                  _const_spec((ngrp, gdim, gout), layer),
                  _const_spec((1, d), layer),
                  _const_spec((d, d), layer)],
        out_specs=xspec,
        out_shape=jax.ShapeDtypeStruct(x.shape, x.dtype),
        scratch_shapes=[pltpu.VMEM((hist + ts, cc), F32),
                        pltpu.VMEM((hist + ts, cp), F32),
                        pltpu.VMEM((ts, cc), BF16),
                        pltpu.VMEM((ts, d), BF16)],
        compiler_params=_params(),
        name=f"mixer_l{layer}",
    )(x, g, w_in, dw_w, dw_b, ln_g, ln_b, w_co, w_pg, p_scale, w_out)


def _xattn_kernel(x_ref, g_ref, wq_ref, k_ref, v_ref, wo_ref, o_ref, abuf):
    ts, d = x_ref.shape
    hd = d // XA_HEADS
    x = x_ref[...]
    hq = _rms(x, g_ref[...]).astype(BF16)
    for hh in range(XA_HEADS):
        cols = slice(hh * hd, (hh + 1) * hd)
        q = (_dot(hq, wq_ref[:, cols]) * (hd ** -0.5)).astype(BF16)
        sc = lax.dot_general(q, k_ref[:, cols], (((1,), (1,)), ((), ())),
                             preferred_element_type=F32)
        p = jnp.exp(sc - jnp.max(sc, axis=-1, keepdims=True))
        l = jnp.sum(p, axis=-1, keepdims=True)
        abuf[:, cols] = (_dot(p.astype(BF16), v_ref[:, cols]) / l).astype(BF16)
    o_ref[...] = x + _dot(abuf[...], wo_ref[...])


def _xattn(x, layer, g, w_q, k, v, w_o):
    b, s, d = x.shape
    m = k.shape[2]
    ts = min(SEQ_TILE, s)
    assert s % ts == 0 and d % XA_HEADS == 0
    xspec = pl.BlockSpec((None, ts, d), lambda i, j: (i, j, 0))
    kvspec = pl.BlockSpec((None, None, m, d), lambda i, j: (layer, i, 0, 0))
    return pl.pallas_call(
        _xattn_kernel,
        grid=(b, s // ts),
        in_specs=[xspec, _const_spec((1, d), layer), _const_spec((d, d), layer),
                  kvspec, kvspec, _const_spec((d, d), layer)],
        out_specs=xspec,
        out_shape=jax.ShapeDtypeStruct(x.shape, x.dtype),
        scratch_shapes=[pltpu.VMEM((ts, d), BF16)],
        compiler_params=_params(),
        name=f"xattn_l{layer}",
    )(x, g, w_q, k, v, w_o)


def _gelu_tanh(x):
    return 0.5 * x * (1.0 + jnp.tanh(0.7978845608028654 * (x + 0.044715 * (x * x * x))))


def _ffn_kernel(x_ref, g_ref, wup_ref, dw_ref, wdn_ref, fg_ref, o_ref, hbuf, ebuf, hist_ref,
                *, chunk, final_norm):
    ts, d = x_ref.shape
    dff = wdn_ref.shape[0]
    kf = dw_ref.shape[0]
    pad = SUBLANES
    s = pl.program_id(1)

    @pl.when(s == 0)
    def _():
        hist_ref[...] = jnp.zeros(hist_ref.shape, F32)

    x = x_ref[...]
    hf = _rms(x, g_ref[...]).astype(BF16)

    def conv_half(c0):
        ebuf[0:pad, :] = hist_ref[:, c0:c0 + chunk]
        ebuf[pad:pad + ts, :] = _dot(hf, wup_ref[:, c0:c0 + chunk])
        hist_ref[:, c0:c0 + chunk] = ebuf[ts:ts + pad, :]
        out = dw_ref[kf - 1:kf, c0:c0 + chunk] * ebuf[pad:pad + ts, :]
        for back in range(1, kf):
            out = out + dw_ref[kf - 1 - back:kf - back, c0:c0 + chunk] * ebuf[pad - back:pad - back + ts, :]
        return out

    for c in range(dff // chunk):
        gate = conv_half(c * chunk)
        val = conv_half(dff + c * chunk)
        hbuf[:, c * chunk:(c + 1) * chunk] = (_gelu_tanh(gate) * val).astype(BF16)

    y = x + _dot(hbuf[...], wdn_ref[...])
    if final_norm:
        y = _rms(y, fg_ref[...])
    o_ref[...] = y


def _ffn(x, layer, g, w_up, dw_w, w_down, final_g, final_norm):
    b, s, d = x.shape
    dff = w_down.shape[1]
    kf = dw_w.shape[1]
    ts = min(SEQ_TILE, s)
    chunk = 256
    assert s % ts == 0 and dff % chunk == 0 and kf - 1 <= SUBLANES
    xspec = pl.BlockSpec((None, ts, d), lambda i, j: (i, j, 0))
    return pl.pallas_call(
        functools.partial(_ffn_kernel, chunk=chunk, final_norm=final_norm),
        grid=(b, s // ts),
        in_specs=[xspec, _const_spec((1, d), layer), _const_spec((d, 2 * dff), layer),
                  _const_spec((kf, 2 * dff), layer), _const_spec((dff, d), layer),
                  _const_spec((1, d))],
        out_specs=xspec,
        out_shape=jax.ShapeDtypeStruct(x.shape, x.dtype),
        scratch_shapes=[pltpu.VMEM((ts, dff), BF16),
                        pltpu.VMEM((SUBLANES + ts, chunk), F32),
                        pltpu.VMEM((SUBLANES, 2 * dff), F32)],
        compiler_params=_params(),
        name=f"ffn_l{layer}",
    )(x, g, w_up, dw_w, w_down, final_g)


def kernel(x, mem, mix_norm_g, w_in, conv_dw_w, conv_dw_b, conv_ln_g, conv_ln_b, w_conv_out,
           w_pool_grp, pool_scale, w_out, xattn_norm_g, mem_norm_g, w_q, w_kv, w_o,
           ffn_norm_g, w_up, ffn_dw_w, w_down, final_norm_g):
    depth = w_in.shape[0]
    d = x.shape[-1]
    row = lambda v: v.reshape(v.shape[0], 1, v.shape[1])
    w_in_bf, w_co_bf, w_pg_bf, w_out_bf = (w.astype(BF16) for w in (w_in, w_conv_out, w_pool_grp, w_out))
    w_q_bf, w_kv_bf, w_o_bf, w_up_bf, w_dn_bf = (w.astype(BF16) for w in (w_q, w_kv, w_o, w_up, w_down))
    k_all, v_all = _memory_kv(mem, mem_norm_g, w_kv_bf)
    final_g = final_norm_g.reshape(1, d)
    for l in range(depth):
        x = _mixer(x, l, row(mix_norm_g), w_in_bf, conv_dw_w, row(conv_dw_b), row(conv_ln_g),
                   row(conv_ln_b), w_co_bf, w_pg_bf, row(pool_scale), w_out_bf)
        x = _xattn(x, l, row(xattn_norm_g), w_q_bf, k_all, v_all, w_o_bf)
        x = _ffn(x, l, row(ffn_norm_g), w_up_bf, ffn_dw_w, w_dn_bf, final_g, l == depth - 1)
    return x
```

```python
import functools

import jax
import jax.numpy as jnp
from jax import lax
from jax.experimental import pallas as pl
from jax.experimental.pallas import tpu as pltpu

EPS = 1e-6
POOL_WINDOWS = (2, 4, 8, 16)
XA_HEADS = 4

F32 = jnp.float32
BF16 = jnp.bfloat16

LANES = 128
SUBLANES = 8
SEQ_TILE = 512
CONV_ROWS = 32
KV_BATCH = 4
VMEM_LIMIT = 52 * 1024 * 1024


def _dot(a, b):
    return jnp.dot(a, b, preferred_element_type=F32)


def _rms(x, g):
    return x * lax.rsqrt(jnp.mean(x * x, axis=-1, keepdims=True) + EPS) * g


def _const_spec(shape, layer=None):
    n = len(shape)
    if layer is None:
        return pl.BlockSpec(shape, lambda b, s: (0,) * n, pipeline_mode=pl.Buffered(1))
    return pl.BlockSpec((None,) + shape, lambda b, s: (layer,) + (0,) * n,
                        pipeline_mode=pl.Buffered(1))


def _params():
    return pltpu.CompilerParams(dimension_semantics=("arbitrary", "arbitrary"),
                                vmem_limit_bytes=VMEM_LIMIT)


def _kv_kernel(mem_ref, g_ref, wkv_ref, kt_ref, v_ref):
    nb, m, d = mem_ref.shape
    mn = _rms(mem_ref[...].reshape(nb * m, d), g_ref[...]).astype(BF16)
    k = _dot(mn, wkv_ref[:, :d])
    v_ref[...] = _dot(mn, wkv_ref[:, d:]).astype(BF16).reshape(nb, m, d)
    for i in range(nb):
        kt_ref[i] = k[i * m:(i + 1) * m, :].T.astype(BF16)


def _memory_kv(mem, mem_norm_g, w_kv_bf):
    depth = w_kv_bf.shape[0]
    b, m, d = mem.shape
    nb = KV_BATCH if b % KV_BATCH == 0 else 1
    return pl.pallas_call(
        _kv_kernel,
        grid=(depth, b // nb),
        in_specs=[pl.BlockSpec((nb, m, d), lambda l, i: (i, 0, 0)),
                  pl.BlockSpec((1, d), lambda l, i: (0, 0)),
                  pl.BlockSpec((None, d, 2 * d), lambda l, i: (l, 0, 0))],
        out_specs=[pl.BlockSpec((None, nb, d, m), lambda l, i: (l, i, 0, 0)),
                   pl.BlockSpec((None, nb, m, d), lambda l, i: (l, i, 0, 0))],
        out_shape=[jax.ShapeDtypeStruct((depth, b, d, m), BF16),
                   jax.ShapeDtypeStruct((depth, b, m, d), BF16)],
        compiler_params=_params(),
        name="memory_kv",
    )(mem, mem_norm_g.reshape(1, d), w_kv_bf)


def _mixer_kernel(x_ref, g_ref, win_ref, dww_ref, dwb_ref, lng_ref, lnb_ref, wco_ref, wpg_ref,
                  ps_ref, wout_ref, o_ref, ybuf, ubuf, cbuf, mbuf, *, hist):
    ts, d = x_ref.shape
    ncb = ybuf.shape[0]
    cc = ncb * LANES
    ngrp, gdim, gout = wpg_ref.shape
    cp = ngrp * gdim
    kc = dww_ref.shape[0]
    s = pl.program_id(1)

    @pl.when(s == 0)
    def _():
        ybuf[:, 0:hist, :] = jnp.zeros((ncb, hist, LANES), F32)
        ubuf[:, 0:hist, :] = jnp.zeros((ngrp, hist, LANES), F32)

    x = x_ref[...]
    h = _rms(x, g_ref[...]).astype(BF16)

    yc = _dot(h, win_ref[:, 0:cc]) * jax.nn.sigmoid(_dot(h, win_ref[:, cc:2 * cc]))
    for j in range(ncb):
        ybuf[j, hist:hist + ts, :] = yc[:, j * LANES:(j + 1) * LANES]
    u = _dot(h, win_ref[:, 2 * cc:2 * cc + cp])
    for j in range(ngrp):
        ubuf[j, hist:hist + ts, :] = u[:, j * LANES:(j + 1) * LANES]

    for i in range(ts // CONV_ROWS):
        r0 = i * CONV_ROWS
        accs = []
        for j in range(ncb):
            lanes = slice(j * LANES, (j + 1) * LANES)
            acc = jnp.zeros((CONV_ROWS, LANES), F32) + dwb_ref[:, lanes]
            for k in range(kc):
                lo = r0 + hist - (kc - 1) + k
                acc = acc + dww_ref[k:k + 1, lanes] * ybuf[j, lo:lo + CONV_ROWS, :]
            accs.append(acc)
        mu = jnp.sum(sum(accs), axis=-1, keepdims=True) * (1.0 / cc)
        xcs = [acc - mu for acc in accs]
        var = jnp.sum(sum(xc * xc for xc in xcs), axis=-1, keepdims=True) * (1.0 / cc)
        inv = lax.rsqrt(var + EPS)
        for j in range(ncb):
            lanes = slice(j * LANES, (j + 1) * LANES)
            y = xcs[j] * inv * lng_ref[:, lanes] + lnb_ref[:, lanes]
            cbuf[r0:r0 + CONV_ROWS, lanes] = (y * jax.nn.sigmoid(y)).astype(BF16)

    t_glob = s * ts + lax.broadcasted_iota(jnp.int32, (ts, LANES), 0)
    g_conv0 = 2 * cc + cp
    g_pool0 = g_conv0 + d
    for j in range(ngrp):
        w = POOL_WINDOWS[j]
        cols = slice(j * gout, (j + 1) * gout)
        cur = ubuf[j, hist:hist + ts, :]
        tot = cur
        for back in range(1, w):
            tot = tot + ubuf[j, hist - back:hist - back + ts, :]
        cnt = jnp.minimum(t_glob + 1, w).astype(F32)
        zp = (tot / cnt - cur).astype(BF16)
        yp = _dot(zp, wpg_ref[j]) * ps_ref[:, cols]
        yo = _dot(cbuf[...], wco_ref[:, cols])
        gc = _dot(h, win_ref[:, g_conv0 + j * gout:g_conv0 + (j + 1) * gout])
        gp = _dot(h, win_ref[:, g_pool0 + j * gout:g_pool0 + (j + 1) * gout])
        mbuf[:, cols] = (jax.nn.sigmoid(gc) * yo + jax.nn.sigmoid(gp) * yp).astype(BF16)

    o_ref[...] = x + _dot(mbuf[...], wout_ref[...])

    ybuf[:, 0:hist, :] = ybuf[:, ts:ts + hist, :]
    ubuf[:, 0:hist, :] = ubuf[:, ts:ts + hist, :]


def _mixer(x, layer, g, w_in, dw_w, dw_b, ln_g, ln_b, w_co, w_pg, p_scale, w_out):
    b, s, d = x.shape
    ts = min(SEQ_TILE, s)
    n_in = w_in.shape[-1]
    kc, cc = dw_w.shape[1:]
    ngrp, gdim, gout = w_pg.shape[1:]
    cp = ngrp * gdim
    hist = -(-max(kc - 1, max(POOL_WINDOWS) - 1) // CONV_ROWS) * CONV_ROWS
    assert s % ts == 0 and ts % CONV_ROWS == 0 and hist <= ts
    assert cc % LANES == 0 and gdim == LANES
    assert n_in == 2 * cc + cp + 2 * d and ngrp == len(POOL_WINDOWS) and ngrp * gout == d
    xspec = pl.BlockSpec((None, ts, d), lambda i, j: (i, j, 0))
    return pl.pallas_call(
        functools.partial(_mixer_kernel, hist=hist),
        grid=(b, s // ts),
        in_specs=[xspec,
                  _const_spec((1, d), layer),
                  _const_spec((d, n_in), layer),
                  _const_spec((kc, cc), layer),
                  _const_spec((1, cc), layer),
                  _const_spec((1, cc), layer),
                  _const_spec((1, cc), layer),
                  _const_spec((cc, d), layer),
                  _const_spec((ngrp, gdim, gout), layer),
                  _const_spec((1, d), layer),
                  _const_spec((d, d), layer)],
        out_specs=xspec,
        out_shape=jax.ShapeDtypeStruct(x.shape, x.dtype),
        scratch_shapes=[pltpu.VMEM((cc // LANES, hist + ts, LANES), F32),
                        pltpu.VMEM((ngrp, hist + ts, LANES), F32),
                        pltpu.VMEM((ts, cc), BF16),
                        pltpu.VMEM((ts, d), BF16)],
        compiler_params=_params(),
        name=f"mixer_l{layer}",
    )(x, g, w_in, dw_w, dw_b, ln_g, ln_b, w_co, w_pg, p_scale, w_out)


def _xattn_kernel(x_ref, g_ref, wq_ref, kt_ref, v_ref, wo_ref, o_ref, abuf):
    ts, d = x_ref.shape
    hd = d // XA_HEADS
    x = x_ref[...]
    hq = _rms(x, g_ref[...]).astype(BF16)
    for hh in range(XA_HEADS):
        cols = slice(hh * hd, (hh + 1) * hd)
        q = (_dot(hq, wq_ref[:, cols]) * (hd ** -0.5)).astype(BF16)
        sc = _dot(q, kt_ref[cols, :])
        p = jnp.exp(sc - jnp.max(sc, axis=-1, keepdims=True))
        l = jnp.sum(p, axis=-1, keepdims=True)
        abuf[:, cols] = (_dot(p.astype(BF16), v_ref[:, cols]) / l).astype(BF16)
    o_ref[...] = x + _dot(abuf[...], wo_ref[...])


def _xattn(x, layer, g, w_q, kt, v, w_o):
    b, s, d = x.shape
    m = v.shape[2]
    ts = min(SEQ_TILE, s)
    assert s % ts == 0 and d % XA_HEADS == 0
    xspec = pl.BlockSpec((None, ts, d), lambda i, j: (i, j, 0))
    return pl.pallas_call(
        _xattn_kernel,
        grid=(b, s // ts),
        in_specs=[xspec, _const_spec((1, d), layer), _const_spec((d, d), layer),
                  pl.BlockSpec((None, None, d, m), lambda i, j: (layer, i, 0, 0)),
                  pl.BlockSpec((None, None, m, d), lambda i, j: (layer, i, 0, 0)),
                  _const_spec((d, d), layer)],
        out_specs=xspec,
        out_shape=jax.ShapeDtypeStruct(x.shape, x.dtype),
        scratch_shapes=[pltpu.VMEM((ts, d), BF16)],
        compiler_params=_params(),
        name=f"xattn_l{layer}",
    )(x, g, w_q, kt, v, w_o)


def _gelu_tanh(x):
    return 0.5 * x * (1.0 + jnp.tanh(0.7978845608028654 * (x + 0.044715 * (x * x * x))))


def _ffn_kernel(x_ref, g_ref, wup_ref, dw_ref, wdn_ref, fg_ref, o_ref, hbuf, ebuf, hist_ref,
                *, final_norm):
    ts, d = x_ref.shape
    dff = wdn_ref.shape[0]
    kf = dw_ref.shape[0]
    ncb = ebuf.shape[0]
    chunk = ncb * LANES
    pad = SUBLANES
    s = pl.program_id(1)

    @pl.when(s == 0)
    def _():
        hist_ref[...] = jnp.zeros(hist_ref.shape, F32)

    x = x_ref[...]
    hf = _rms(x, g_ref[...]).astype(BF16)

    def conv_half(c0):
        up = _dot(hf, wup_ref[:, c0:c0 + chunk])
        outs = []
        for j in range(ncb):
            lanes = slice(c0 + j * LANES, c0 + (j + 1) * LANES)
            ebuf[j, 0:pad, :] = hist_ref[:, lanes]
            ebuf[j, pad:pad + ts, :] = up[:, j * LANES:(j + 1) * LANES]
            hist_ref[:, lanes] = ebuf[j, ts:ts + pad, :]
            out = dw_ref[kf - 1:kf, lanes] * up[:, j * LANES:(j + 1) * LANES]
            for back in range(1, kf):
                out = out + dw_ref[kf - 1 - back:kf - back, lanes] * ebuf[j, pad - back:pad - back + ts, :]
            outs.append(out)
        return jnp.concatenate(outs, axis=-1)

    for c in range(dff // chunk):
        gate = conv_half(c * chunk)
        val = conv_half(dff + c * chunk)
        hbuf[:, c * chunk:(c + 1) * chunk] = (_gelu_tanh(gate) * val).astype(BF16)

    y = x + _dot(hbuf[...], wdn_ref[...])
    if final_norm:
        y = _rms(y, fg_ref[...])
    o_ref[...] = y


def _ffn(x, layer, g, w_up, dw_w, w_down, final_g, final_norm):
    b, s, d = x.shape
    dff = w_down.shape[1]
    kf = dw_w.shape[1]
    ts = min(SEQ_TILE, s)
    chunk = 2 * LANES
    assert s % ts == 0 and dff % chunk == 0 and kf - 1 <= SUBLANES
    xspec = pl.BlockSpec((None, ts, d), lambda i, j: (i, j, 0))
    return pl.pallas_call(
        functools.partial(_ffn_kernel, final_norm=final_norm),
        grid=(b, s // ts),
        in_specs=[xspec, _const_spec((1, d), layer), _const_spec((d, 2 * dff), layer),
                  _const_spec((kf, 2 * dff), layer), _const_spec((dff, d), layer),
                  _const_spec((1, d))],
        out_specs=xspec,
        out_shape=jax.ShapeDtypeStruct(x.shape, x.dtype),
        scratch_shapes=[pltpu.VMEM((ts, dff), BF16),
                        pltpu.VMEM((chunk // LANES, SUBLANES + ts, LANES), F32),
                        pltpu.VMEM((SUBLANES, 2 * dff), F32)],
        compiler_params=_params(),
        name=f"ffn_l{layer}",
    )(x, g, w_up, dw_w, w_down, final_g)


def kernel(x, mem, mix_norm_g, w_in, conv_dw_w, conv_dw_b, conv_ln_g, conv_ln_b, w_conv_out,
           w_pool_grp, pool_scale, w_out, xattn_norm_g, mem_norm_g, w_q, w_kv, w_o,
           ffn_norm_g, w_up, ffn_dw_w, w_down, final_norm_g):
    depth = w_in.shape[0]
    d = x.shape[-1]
    row = lambda v: v.reshape(v.shape[0], 1, v.shape[1])
    w_in_bf, w_co_bf, w_pg_bf, w_out_bf = (w.astype(BF16) for w in (w_in, w_conv_out, w_pool_grp, w_out))
    w_q_bf, w_kv_bf, w_o_bf, w_up_bf, w_dn_bf = (w.astype(BF16) for w in (w_q, w_kv, w_o, w_up, w_down))
    kt_all, v_all = _memory_kv(mem, mem_norm_g, w_kv_bf)
    final_g = final_norm_g.reshape(1, d)
    for l in range(depth):
        x = _mixer(x, l, row(mix_norm_g), w_in_bf, conv_dw_w, row(conv_dw_b), row(conv_ln_g),
                   row(conv_ln_b), w_co_bf, w_pg_bf, row(pool_scale), w_out_bf)
        x = _xattn(x, l, row(xattn_norm_g), w_q_bf, kt_all, v_all, w_o_bf)
        x = _ffn(x, l, row(ffn_norm_g), w_up_bf, ffn_dw_w, w_dn_bf, final_g, l == depth - 1)
    return x
```

```python
import functools

import jax
import jax.numpy as jnp
from jax import lax
from jax.experimental import pallas as pl
from jax.experimental.pallas import tpu as pltpu

EPS = 1e-6
POOL_WINDOWS = (2, 4, 8, 16)
XA_HEADS = 4

F32 = jnp.float32
BF16 = jnp.bfloat16
U32 = jnp.uint32

LANES = 128
SUBLANES = 8
BF16_ROWS = 16
SEQ_TILE = 512
CONV_ROWS = 128
KV_BATCH = 4
VMEM_LIMIT = 52 * 1024 * 1024


def _dot(a, b):
    return jnp.dot(a, b, preferred_element_type=F32)


def _rms(x, g):
    return x * lax.rsqrt(jnp.mean(x * x, axis=-1, keepdims=True) + EPS) * g


def _const_spec(shape, layer=None):
    n = len(shape)
    if layer is None:
        return pl.BlockSpec(shape, lambda b, s: (0,) * n, pipeline_mode=pl.Buffered(1))
    return pl.BlockSpec((None,) + shape, lambda b, s: (layer,) + (0,) * n,
                        pipeline_mode=pl.Buffered(1))


def _params():
    return pltpu.CompilerParams(dimension_semantics=("arbitrary", "arbitrary"),
                                vmem_limit_bytes=VMEM_LIMIT)


def _kv_kernel(mem_ref, g_ref, wkv_ref, kt_ref, v_ref):
    nb, m, d = mem_ref.shape
    mn = _rms(mem_ref[...].reshape(nb * m, d), g_ref[...]).astype(BF16)
    k = _dot(mn, wkv_ref[:, :d])
    v_ref[...] = _dot(mn, wkv_ref[:, d:]).astype(BF16).reshape(nb, m, d)
    for i in range(nb):
        kt_ref[i] = k[i * m:(i + 1) * m, :].T.astype(BF16)


def _memory_kv(mem, mem_norm_g, w_kv_bf):
    depth = w_kv_bf.shape[0]
    b, m, d = mem.shape
    nb = KV_BATCH if b % KV_BATCH == 0 else 1
    return pl.pallas_call(
        _kv_kernel,
        grid=(depth, b // nb),
        in_specs=[pl.BlockSpec((nb, m, d), lambda l, i: (i, 0, 0)),
                  pl.BlockSpec((1, d), lambda l, i: (0, 0)),
                  pl.BlockSpec((None, d, 2 * d), lambda l, i: (l, 0, 0))],
        out_specs=[pl.BlockSpec((None, nb, d, m), lambda l, i: (l, i, 0, 0)),
                   pl.BlockSpec((None, nb, m, d), lambda l, i: (l, i, 0, 0))],
        out_shape=[jax.ShapeDtypeStruct((depth, b, d, m), BF16),
                   jax.ShapeDtypeStruct((depth, b, m, d), BF16)],
        compiler_params=_params(),
        name="memory_kv",
    )(mem, mem_norm_g.reshape(1, d), w_kv_bf)


def _mixer_kernel(x_ref, g_ref, win_ref, dww_ref, dwb_ref, lng_ref, lnb_ref, wco_ref, wpg_ref,
                  ps_ref, wout_ref, o_ref, ybuf, ebuf, obuf, wbuf, tbuf, ubuf, cbuf, mbuf, *, hist):
    ts, d = x_ref.shape
    ncb = ybuf.shape[0]
    cc = ncb * LANES
    rows = hist + ts
    ngrp, gdim, gout = wpg_ref.shape
    cp = ngrp * gdim
    kc = dww_ref.shape[0]
    s = pl.program_id(1)

    @pl.when(s == 0)
    def _():
        ybuf[:, 0:hist, :] = jnp.zeros((ncb, hist, LANES), F32)
        ybuf[:, rows:rows + BF16_ROWS, :] = jnp.zeros((ncb, BF16_ROWS, LANES), F32)
        ubuf[:, 0:hist, :] = jnp.zeros((ngrp, hist, LANES), F32)
        for k in range(kc):
            wbuf[k] = jnp.broadcast_to(dww_ref[k:k + 1, :], (BF16_ROWS, cc)).astype(BF16)

    x = x_ref[...]
    h = _rms(x, g_ref[...]).astype(BF16)

    yc = _dot(h, win_ref[:, 0:cc]) * jax.nn.sigmoid(_dot(h, win_ref[:, cc:2 * cc]))
    for j in range(ncb):
        ybuf[j, hist:rows, :] = yc[:, j * LANES:(j + 1) * LANES]
    u = _dot(h, win_ref[:, 2 * cc:2 * cc + cp])
    for j in range(ngrp):
        ubuf[j, hist:rows, :] = u[:, j * LANES:(j + 1) * LANES]

    for j in range(ncb):
        ebuf[j] = pltpu.bitcast(ybuf[j, 0:rows, :].astype(BF16), U32)
        obuf[j] = pltpu.bitcast(ybuf[j, 1:rows + 1, :].astype(BF16), U32)

    for j in range(ncb):
        lanes = slice(j * LANES, (j + 1) * LANES)
        taps = [jnp.concatenate([wbuf[k, :, lanes]] * (CONV_ROWS // BF16_ROWS), axis=0)
                for k in range(kc)]

        def conv_rows(n, carry, j=j, lanes=lanes, taps=taps):
            r0 = pl.multiple_of(n * CONV_ROWS, CONV_ROWS)
            w0 = pl.multiple_of(n * (CONV_ROWS // 2), CONV_ROWS // 2)
            acc = jnp.zeros((CONV_ROWS, LANES), F32)
            for k in range(kc):
                lo = hist - (kc - 1) + k
                src = ebuf if lo % 2 == 0 else obuf
                win = pltpu.bitcast(src[j, pl.ds(w0 + lo // 2, CONV_ROWS // 2), :], BF16)
                acc = acc + win.astype(F32) * taps[k].astype(F32)
            tbuf[pl.ds(r0, CONV_ROWS), lanes] = acc
            return carry

        lax.fori_loop(0, ts // CONV_ROWS, conv_rows, 0)

    yv = tbuf[...] + dwb_ref[...]
    xc = yv - jnp.mean(yv, axis=-1, keepdims=True)
    var = jnp.mean(xc * xc, axis=-1, keepdims=True)
    yn = xc * lax.rsqrt(var + EPS) * lng_ref[...] + lnb_ref[...]
    cbuf[...] = (yn * jax.nn.sigmoid(yn)).astype(BF16)

    t_glob = s * ts + lax.broadcasted_iota(jnp.int32, (ts, LANES), 0)
    g_conv0 = 2 * cc + cp
    g_pool0 = g_conv0 + d
    for j in range(ngrp):
        w = POOL_WINDOWS[j]
        cols = slice(j * gout, (j + 1) * gout)
        cur = ubuf[j, hist:rows, :]
        tot = cur
        for back in range(1, w):
            tot = tot + ubuf[j, hist - back:rows - back, :]
        cnt = jnp.minimum(t_glob + 1, w).astype(F32)
        zp = (tot / cnt - cur).astype(BF16)
        yp = _dot(zp, wpg_ref[j]) * ps_ref[:, cols]
        yo = _dot(cbuf[...], wco_ref[:, cols])
        gc = _dot(h, win_ref[:, g_conv0 + j * gout:g_conv0 + (j + 1) * gout])
        gp = _dot(h, win_ref[:, g_pool0 + j * gout:g_pool0 + (j + 1) * gout])
        mbuf[:, cols] = (jax.nn.sigmoid(gc) * yo + jax.nn.sigmoid(gp) * yp).astype(BF16)

    o_ref[...] = x + _dot(mbuf[...], wout_ref[...])

    ybuf[:, 0:hist, :] = ybuf[:, ts:rows, :]
    ubuf[:, 0:hist, :] = ubuf[:, ts:rows, :]


def _mixer(x, layer, g, w_in, dw_w, dw_b, ln_g, ln_b, w_co, w_pg, p_scale, w_out):
    b, s, d = x.shape
    ts = min(SEQ_TILE, s)
    n_in = w_in.shape[-1]
    kc, cc = dw_w.shape[1:]
    ngrp, gdim, gout = w_pg.shape[1:]
    cp = ngrp * gdim
    hist = -(-max(kc - 1, max(POOL_WINDOWS) - 1) // BF16_ROWS) * BF16_ROWS
    assert s % ts == 0 and ts % CONV_ROWS == 0 and CONV_ROWS % BF16_ROWS == 0 and hist <= ts
    assert cc % LANES == 0 and gdim == LANES
    assert n_in == 2 * cc + cp + 2 * d and ngrp == len(POOL_WINDOWS) and ngrp * gout == d
    xspec = pl.BlockSpec((None, ts, d), lambda i, j: (i, j, 0))
    return pl.pallas_call(
        functools.partial(_mixer_kernel, hist=hist),
        grid=(b, s // ts),
        in_specs=[xspec,
                  _const_spec((1, d), layer),
                  _const_spec((d, n_in), layer),
                  _const_spec((kc, cc), layer),
                  _const_spec((1, cc), layer),
                  _const_spec((1, cc), layer),
                  _const_spec((1, cc), layer),
                  _const_spec((cc, d), layer),
                  _const_spec((ngrp, gdim, gout), layer),
                  _const_spec((1, d), layer),
                  _const_spec((d, d), layer)],
        out_specs=xspec,
        out_shape=jax.ShapeDtypeStruct(x.shape, x.dtype),
        scratch_shapes=[pltpu.VMEM((cc // LANES, hist + ts + BF16_ROWS, LANES), F32),
                        pltpu.VMEM((cc // LANES, (hist + ts) // 2, LANES), U32),
                        pltpu.VMEM((cc // LANES, (hist + ts) // 2, LANES), U32),
                        pltpu.VMEM((kc, BF16_ROWS, cc), BF16),
                        pltpu.VMEM((ts, cc), F32),
                        pltpu.VMEM((ngrp, hist + ts, LANES), F32),
                        pltpu.VMEM((ts, cc), BF16),
                        pltpu.VMEM((ts, d), BF16)],
        compiler_params=_params(),
        name=f"mixer_l{layer}",
    )(x, g, w_in, dw_w, dw_b, ln_g, ln_b, w_co, w_pg, p_scale, w_out)


def _xattn_kernel(x_ref, g_ref, wq_ref, kt_ref, v_ref, wo_ref, o_ref, abuf):
    ts, d = x_ref.shape
    hd = d // XA_HEADS
    x = x_ref[...]
    hq = _rms(x, g_ref[...]).astype(BF16)
    q = (_dot(hq, wq_ref[...]) * (hd ** -0.5)).astype(BF16)
    heads = [slice(hh * hd, (hh + 1) * hd) for hh in range(XA_HEADS)]
    scores = [_dot(q[:, cols], kt_ref[cols, :]) for cols in heads]
    for cols, sc in zip(heads, scores):
        p = jnp.exp(sc - jnp.max(sc, axis=-1, keepdims=True))
        l = jnp.sum(p, axis=-1, keepdims=True)
        abuf[:, cols] = (_dot(p.astype(BF16), v_ref[:, cols]) / l).astype(BF16)
    o_ref[...] = x + _dot(abuf[...], wo_ref[...])


def _xattn(x, layer, g, w_q, kt, v, w_o):
    b, s, d = x.shape
    m = v.shape[2]
    ts = min(SEQ_TILE, s)
    assert s % ts == 0 and d % XA_HEADS == 0
    xspec = pl.BlockSpec((None, ts, d), lambda i, j: (i, j, 0))
    return pl.pallas_call(
        _xattn_kernel,
        grid=(b, s // ts),
        in_specs=[xspec, _const_spec((1, d), layer), _const_spec((d, d), layer),
                  pl.BlockSpec((None, None, d, m), lambda i, j: (layer, i, 0, 0)),
                  pl.BlockSpec((None, None, m, d), lambda i, j: (layer, i, 0, 0)),
                  _const_spec((d, d), layer)],
        out_specs=xspec,
        out_shape=jax.ShapeDtypeStruct(x.shape, x.dtype),
        scratch_shapes=[pltpu.VMEM((ts, d), BF16)],
        compiler_params=_params(),
        name=f"xattn_l{layer}",
    )(x, g, w_q, kt, v, w_o)


def _gelu_tanh(x):
    return 0.5 * x * (1.0 + jnp.tanh(0.7978845608028654 * (x + 0.044715 * (x * x * x))))


def _ffn_kernel(x_ref, g_ref, wup_ref, dw_ref, wdn_ref, fg_ref, o_ref, hbuf, ebuf, hist_ref,
                *, chunk, final_norm):
    ts, d = x_ref.shape
    dff = wdn_ref.shape[0]
    kf = dw_ref.shape[0]
    pad = SUBLANES
    s = pl.program_id(1)

    @pl.when(s == 0)
    def _():
        hist_ref[...] = jnp.zeros(hist_ref.shape, F32)

    x = x_ref[...]
    hf = _rms(x, g_ref[...]).astype(BF16)

    def conv_half(c0):
        ebuf[0:pad, :] = hist_ref[:, c0:c0 + chunk]
        ebuf[pad:pad + ts, :] = _dot(hf, wup_ref[:, c0:c0 + chunk])
        hist_ref[:, c0:c0 + chunk] = ebuf[ts:ts + pad, :]
        out = dw_ref[kf - 1:kf, c0:c0 + chunk] * ebuf[pad:pad + ts, :]
        for back in range(1, kf):
            out = out + dw_ref[kf - 1 - back:kf - back, c0:c0 + chunk] * ebuf[pad - back:pad - back + ts, :]
        return out

    for c in range(dff // chunk):
        gate = conv_half(c * chunk)
        val = conv_half(dff + c * chunk)
        hbuf[:, c * chunk:(c + 1) * chunk] = (_gelu_tanh(gate) * val).astype(BF16)

    y = x + _dot(hbuf[...], wdn_ref[...])
    if final_norm:
        y = _rms(y, fg_ref[...])
    o_ref[...] = y


def _ffn(x, layer, g, w_up, dw_w, w_down, final_g, final_norm):
    b, s, d = x.shape
    dff = w_down.shape[1]
    kf = dw_w.shape[1]
    ts = min(SEQ_TILE, s)
    chunk = 2 * LANES
    assert s % ts == 0 and dff % chunk == 0 and kf - 1 <= SUBLANES
    xspec = pl.BlockSpec((None, ts, d), lambda i, j: (i, j, 0))
    return pl.pallas_call(
        functools.partial(_ffn_kernel, chunk=chunk, final_norm=final_norm),
        grid=(b, s // ts),
        in_specs=[xspec, _const_spec((1, d), layer), _const_spec((d, 2 * dff), layer),
                  _const_spec((kf, 2 * dff), layer), _const_spec((dff, d), layer),
                  _const_spec((1, d))],
        out_specs=xspec,
        out_shape=jax.ShapeDtypeStruct(x.shape, x.dtype),
        scratch_shapes=[pltpu.VMEM((ts, dff), BF16),
                        pltpu.VMEM((SUBLANES + ts, chunk), F32),
                        pltpu.VMEM((SUBLANES, 2 * dff), F32)],
        compiler_params=_params(),
        name=f"ffn_l{layer}",
    )(x, g, w_up, dw_w, w_down, final_g)


def kernel(x, mem, mix_norm_g, w_in, conv_dw_w, conv_dw_b, conv_ln_g, conv_ln_b, w_conv_out,
           w_pool_grp, pool_scale, w_out, xattn_norm_g, mem_norm_g, w_q, w_kv, w_o,
           ffn_norm_g, w_up, ffn_dw_w, w_down, final_norm_g):
    depth = w_in.shape[0]
    d = x.shape[-1]
    row = lambda v: v.reshape(v.shape[0], 1, v.shape[1])
    w_in_bf, w_co_bf, w_pg_bf, w_out_bf = (w.astype(BF16) for w in (w_in, w_conv_out, w_pool_grp, w_out))
    w_q_bf, w_kv_bf, w_o_bf, w_up_bf, w_dn_bf = (w.astype(BF16) for w in (w_q, w_kv, w_o, w_up, w_down))
    kt_all, v_all = _memory_kv(mem, mem_norm_g, w_kv_bf)
    final_g = final_norm_g.reshape(1, d)
    for l in range(depth):
        x = _mixer(x, l, row(mix_norm_g), w_in_bf, conv_dw_w, row(conv_dw_b), row(conv_ln_g),
                   row(conv_ln_b), w_co_bf, w_pg_bf, row(pool_scale), w_out_bf)
        x = _xattn(x, l, row(xattn_norm_g), w_q_bf, kt_all, v_all, w_o_bf)
        x = _ffn(x, l, row(ffn_norm_g), w_up_bf, ffn_dw_w, w_dn_bf, final_g, l == depth - 1)
    return x
```

```python
import functools

import jax
import jax.numpy as jnp
from jax import lax
from jax.experimental import pallas as pl
from jax.experimental.pallas import tpu as pltpu

EPS = 1e-6
POOL_WINDOWS = (2, 4, 8, 16)
XA_HEADS = 4

F32 = jnp.float32
BF16 = jnp.bfloat16
U32 = jnp.uint32

LANES = 128
SUBLANES = 8
BF16_ROWS = 16
SEQ_TILE = 1024
FFN_SEQ_TILE = 512
CONV_ROWS = 128
KV_BATCH = 4
VMEM_LIMIT = 52 * 1024 * 1024


def _dot(a, b):
    return jnp.dot(a, b, preferred_element_type=F32)


def _rms(x, g):
    return x * lax.rsqrt(jnp.mean(x * x, axis=-1, keepdims=True) + EPS) * g


def _const_spec(shape, layer=None):
    n = len(shape)
    if layer is None:
        return pl.BlockSpec(shape, lambda b, s: (0,) * n, pipeline_mode=pl.Buffered(1))
    return pl.BlockSpec((None,) + shape, lambda b, s: (layer,) + (0,) * n,
                        pipeline_mode=pl.Buffered(1))


def _params():
    return pltpu.CompilerParams(dimension_semantics=("arbitrary", "arbitrary"),
                                vmem_limit_bytes=VMEM_LIMIT)


def _kv_kernel(mem_ref, g_ref, wkv_ref, kt_ref, v_ref):
    nb, m, d = mem_ref.shape
    mn = _rms(mem_ref[...].reshape(nb * m, d), g_ref[...]).astype(BF16)
    k = _dot(mn, wkv_ref[:, :d])
    v_ref[...] = _dot(mn, wkv_ref[:, d:]).astype(BF16).reshape(nb, m, d)
    for i in range(nb):
        kt_ref[i] = k[i * m:(i + 1) * m, :].T.astype(BF16)


def _memory_kv(mem, mem_norm_g, w_kv_bf):
    depth = w_kv_bf.shape[0]
    b, m, d = mem.shape
    nb = KV_BATCH if b % KV_BATCH == 0 else 1
    return pl.pallas_call(
        _kv_kernel,
        grid=(depth, b // nb),
        in_specs=[pl.BlockSpec((nb, m, d), lambda l, i: (i, 0, 0)),
                  pl.BlockSpec((1, d), lambda l, i: (0, 0)),
                  pl.BlockSpec((None, d, 2 * d), lambda l, i: (l, 0, 0))],
        out_specs=[pl.BlockSpec((None, nb, d, m), lambda l, i: (l, i, 0, 0)),
                   pl.BlockSpec((None, nb, m, d), lambda l, i: (l, i, 0, 0))],
        out_shape=[jax.ShapeDtypeStruct((depth, b, d, m), BF16),
                   jax.ShapeDtypeStruct((depth, b, m, d), BF16)],
        compiler_params=_params(),
        name="memory_kv",
    )(mem, mem_norm_g.reshape(1, d), w_kv_bf)


def _mixer_kernel(x_ref, g_ref, win_ref, dww_ref, dwb_ref, lng_ref, lnb_ref, wco_ref, wpg_ref,
                  ps_ref, wout_ref, o_ref, ybuf, ebuf, obuf, wbuf, tbuf, ubuf, cbuf, mbuf, *, hist):
    ts, d = x_ref.shape
    ncb = ybuf.shape[0]
    cc = ncb * LANES
    rows = hist + ts
    ngrp, gdim, gout = wpg_ref.shape
    cp = ngrp * gdim
    kc = dww_ref.shape[0]
    s = pl.program_id(1)

    @pl.when(s == 0)
    def _():
        ybuf[:, 0:hist, :] = jnp.zeros((ncb, hist, LANES), F32)
        ybuf[:, rows:rows + BF16_ROWS, :] = jnp.zeros((ncb, BF16_ROWS, LANES), F32)
        ubuf[:, 0:hist, :] = jnp.zeros((ngrp, hist, LANES), F32)
        for k in range(kc):
            wbuf[k] = jnp.broadcast_to(dww_ref[k:k + 1, :], (BF16_ROWS, cc)).astype(BF16)

    x = x_ref[...]
    h = _rms(x, g_ref[...]).astype(BF16)

    yc = _dot(h, win_ref[:, 0:cc]) * jax.nn.sigmoid(_dot(h, win_ref[:, cc:2 * cc]))
    for j in range(ncb):
        ybuf[j, hist:rows, :] = yc[:, j * LANES:(j + 1) * LANES]
    u = _dot(h, win_ref[:, 2 * cc:2 * cc + cp])
    for j in range(ngrp):
        ubuf[j, hist:rows, :] = u[:, j * LANES:(j + 1) * LANES]

    for j in range(ncb):
        ebuf[j] = pltpu.bitcast(ybuf[j, 0:rows, :].astype(BF16), U32)
        obuf[j] = pltpu.bitcast(ybuf[j, 1:rows + 1, :].astype(BF16), U32)

    for j in range(ncb):
        lanes = slice(j * LANES, (j + 1) * LANES)
        taps = [jnp.concatenate([wbuf[k, :, lanes]] * (CONV_ROWS // BF16_ROWS), axis=0)
                for k in range(kc)]

        def conv_rows(n, carry, j=j, lanes=lanes, taps=taps):
            r0 = pl.multiple_of(n * CONV_ROWS, CONV_ROWS)
            w0 = pl.multiple_of(n * (CONV_ROWS // 2), CONV_ROWS // 2)
            acc = jnp.zeros((CONV_ROWS, LANES), F32)
            for k in range(kc):
                lo = hist - (kc - 1) + k
                src = ebuf if lo % 2 == 0 else obuf
                win = pltpu.bitcast(src[j, pl.ds(w0 + lo // 2, CONV_ROWS // 2), :], BF16)
                acc = acc + win.astype(F32) * taps[k].astype(F32)
            tbuf[pl.ds(r0, CONV_ROWS), lanes] = acc
            return carry

        lax.fori_loop(0, ts // CONV_ROWS, conv_rows, 0)

    yv = tbuf[...] + dwb_ref[...]
    xc = yv - jnp.mean(yv, axis=-1, keepdims=True)
    var = jnp.mean(xc * xc, axis=-1, keepdims=True)
    yn = xc * lax.rsqrt(var + EPS) * lng_ref[...] + lnb_ref[...]
    cbuf[...] = (yn * jax.nn.sigmoid(yn)).astype(BF16)

    t_glob = s * ts + lax.broadcasted_iota(jnp.int32, (ts, LANES), 0)
    g_conv0 = 2 * cc + cp
    g_pool0 = g_conv0 + d
    for j in range(ngrp):
        w = POOL_WINDOWS[j]
        cols = slice(j * gout, (j + 1) * gout)
        cur = ubuf[j, hist:rows, :]
        tot = cur
        for back in range(1, w):
            tot = tot + ubuf[j, hist - back:rows - back, :]
        cnt = jnp.minimum(t_glob + 1, w).astype(F32)
        zp = (tot / cnt - cur).astype(BF16)
        yp = _dot(zp, wpg_ref[j]) * ps_ref[:, cols]
        yo = _dot(cbuf[...], wco_ref[:, cols])
        gc = _dot(h, win_ref[:, g_conv0 + j * gout:g_conv0 + (j + 1) * gout])
        gp = _dot(h, win_ref[:, g_pool0 + j * gout:g_pool0 + (j + 1) * gout])
        mbuf[:, cols] = (jax.nn.sigmoid(gc) * yo + jax.nn.sigmoid(gp) * yp).astype(BF16)

    o_ref[...] = x + _dot(mbuf[...], wout_ref[...])

    ybuf[:, 0:hist, :] = ybuf[:, ts:rows, :]
    ubuf[:, 0:hist, :] = ubuf[:, ts:rows, :]


def _mixer(x, layer, g, w_in, dw_w, dw_b, ln_g, ln_b, w_co, w_pg, p_scale, w_out):
    b, s, d = x.shape
    ts = min(SEQ_TILE, s)
    n_in = w_in.shape[-1]
    kc, cc = dw_w.shape[1:]
    ngrp, gdim, gout = w_pg.shape[1:]
    cp = ngrp * gdim
    hist = -(-max(kc - 1, max(POOL_WINDOWS) - 1) // BF16_ROWS) * BF16_ROWS
    assert s % ts == 0 and ts % CONV_ROWS == 0 and CONV_ROWS % BF16_ROWS == 0 and hist <= ts
    assert cc % LANES == 0 and gdim == LANES
    assert n_in == 2 * cc + cp + 2 * d and ngrp == len(POOL_WINDOWS) and ngrp * gout == d
    xspec = pl.BlockSpec((None, ts, d), lambda i, j: (i, j, 0))
    return pl.pallas_call(
        functools.partial(_mixer_kernel, hist=hist),
        grid=(b, s // ts),
        in_specs=[xspec,
                  _const_spec((1, d), layer),
                  _const_spec((d, n_in), layer),
                  _const_spec((kc, cc), layer),
                  _const_spec((1, cc), layer),
                  _const_spec((1, cc), layer),
                  _const_spec((1, cc), layer),
                  _const_spec((cc, d), layer),
                  _const_spec((ngrp, gdim, gout), layer),
                  _const_spec((1, d), layer),
                  _const_spec((d, d), layer)],
        out_specs=xspec,
        out_shape=jax.ShapeDtypeStruct(x.shape, x.dtype),
        scratch_shapes=[pltpu.VMEM((cc // LANES, hist + ts + BF16_ROWS, LANES), F32),
                        pltpu.VMEM((cc // LANES, (hist + ts) // 2, LANES), U32),
                        pltpu.VMEM((cc // LANES, (hist + ts) // 2, LANES), U32),
                        pltpu.VMEM((kc, BF16_ROWS, cc), BF16),
                        pltpu.VMEM((ts, cc), F32),
                        pltpu.VMEM((ngrp, hist + ts, LANES), F32),
                        pltpu.VMEM((ts, cc), BF16),
                        pltpu.VMEM((ts, d), BF16)],
        compiler_params=_params(),
        name=f"mixer_l{layer}",
    )(x, g, w_in, dw_w, dw_b, ln_g, ln_b, w_co, w_pg, p_scale, w_out)


def _xattn_kernel(x_ref, g_ref, wq_ref, kt_ref, v_ref, wo_ref, o_ref, abuf):
    ts, d = x_ref.shape
    hd = d // XA_HEADS
    x = x_ref[...]
    hq = _rms(x, g_ref[...]).astype(BF16)
    q = (_dot(hq, wq_ref[...]) * (hd ** -0.5)).astype(BF16)
    heads = [slice(hh * hd, (hh + 1) * hd) for hh in range(XA_HEADS)]
    scores = [_dot(q[:, cols], kt_ref[cols, :]) for cols in heads]
    for cols, sc in zip(heads, scores):
        p = jnp.exp(sc - jnp.max(sc, axis=-1, keepdims=True))
        l = jnp.sum(p, axis=-1, keepdims=True)
        abuf[:, cols] = (_dot(p.astype(BF16), v_ref[:, cols]) / l).astype(BF16)
    o_ref[...] = x + _dot(abuf[...], wo_ref[...])


def _xattn(x, layer, g, w_q, kt, v, w_o):
    b, s, d = x.shape
    m = v.shape[2]
    ts = min(SEQ_TILE, s)
    assert s % ts == 0 and d % XA_HEADS == 0
    xspec = pl.BlockSpec((None, ts, d), lambda i, j: (i, j, 0))
    return pl.pallas_call(
        _xattn_kernel,
        grid=(b, s // ts),
        in_specs=[xspec, _const_spec((1, d), layer), _const_spec((d, d), layer),
                  pl.BlockSpec((None, None, d, m), lambda i, j: (layer, i, 0, 0)),
                  pl.BlockSpec((None, None, m, d), lambda i, j: (layer, i, 0, 0)),
                  _const_spec((d, d), layer)],
        out_specs=xspec,
        out_shape=jax.ShapeDtypeStruct(x.shape, x.dtype),
        scratch_shapes=[pltpu.VMEM((ts, d), BF16)],
        compiler_params=_params(),
        name=f"xattn_l{layer}",
    )(x, g, w_q, kt, v, w_o)


def _gelu_tanh(x):
    return 0.5 * x * (1.0 + jnp.tanh(0.7978845608028654 * (x + 0.044715 * (x * x * x))))


def _ffn_kernel(x_ref, g_ref, wup_ref, dw_ref, wdn_ref, fg_ref, o_ref, hbuf, ebuf, hist_ref,
                *, chunk, final_norm):
    ts, d = x_ref.shape
    dff = wdn_ref.shape[0]
    kf = dw_ref.shape[0]
    pad = SUBLANES
    s = pl.program_id(1)

    @pl.when(s == 0)
    def _():
        hist_ref[...] = jnp.zeros(hist_ref.shape, F32)

    x = x_ref[...]
    hf = _rms(x, g_ref[...]).astype(BF16)

    def conv_half(c0):
        ebuf[0:pad, :] = hist_ref[:, c0:c0 + chunk]
        ebuf[pad:pad + ts, :] = _dot(hf, wup_ref[:, c0:c0 + chunk])
        hist_ref[:, c0:c0 + chunk] = ebuf[ts:ts + pad, :]
        out = dw_ref[kf - 1:kf, c0:c0 + chunk] * ebuf[pad:pad + ts, :]
        for back in range(1, kf):
            out = out + dw_ref[kf - 1 - back:kf - back, c0:c0 + chunk] * ebuf[pad - back:pad - back + ts, :]
        return out

    for c in range(dff // chunk):
        gate = conv_half(c * chunk)
        val = conv_half(dff + c * chunk)
        hbuf[:, c * chunk:(c + 1) * chunk] = (_gelu_tanh(gate) * val).astype(BF16)

    y = x + _dot(hbuf[...], wdn_ref[...])
    if final_norm:
        y = _rms(y, fg_ref[...])
    o_ref[...] = y


def _ffn(x, layer, g, w_up, dw_w, w_down, final_g, final_norm):
    b, s, d = x.shape
    dff = w_down.shape[1]
    kf = dw_w.shape[1]
    ts = min(FFN_SEQ_TILE, s)
    chunk = 2 * LANES
    assert s % ts == 0 and dff % chunk == 0 and kf - 1 <= SUBLANES
    xspec = pl.BlockSpec((None, ts, d), lambda i, j: (i, j, 0))
    return pl.pallas_call(
        functools.partial(_ffn_kernel, chunk=chunk, final_norm=final_norm),
        grid=(b, s // ts),
        in_specs=[xspec, _const_spec((1, d), layer), _const_spec((d, 2 * dff), layer),
                  _const_spec((kf, 2 * dff), layer), _const_spec((dff, d), layer),
                  _const_spec((1, d))],
        out_specs=xspec,
        out_shape=jax.ShapeDtypeStruct(x.shape, x.dtype),
        scratch_shapes=[pltpu.VMEM((ts, dff), BF16),
                        pltpu.VMEM((SUBLANES + ts, chunk), F32),
                        pltpu.VMEM((SUBLANES, 2 * dff), F32)],
        compiler_params=_params(),
        name=f"ffn_l{layer}",
    )(x, g, w_up, dw_w, w_down, final_g)


def kernel(x, mem, mix_norm_g, w_in, conv_dw_w, conv_dw_b, conv_ln_g, conv_ln_b, w_conv_out,
           w_pool_grp, pool_scale, w_out, xattn_norm_g, mem_norm_g, w_q, w_kv, w_o,
           ffn_norm_g, w_up, ffn_dw_w, w_down, final_norm_g):
    depth = w_in.shape[0]
    d = x.shape[-1]
    row = lambda v: v.reshape(v.shape[0], 1, v.shape[1])
    w_in_bf, w_co_bf, w_pg_bf, w_out_bf = (w.astype(BF16) for w in (w_in, w_conv_out, w_pool_grp, w_out))
    w_q_bf, w_kv_bf, w_o_bf, w_up_bf, w_dn_bf = (w.astype(BF16) for w in (w_q, w_kv, w_o, w_up, w_down))
    kt_all, v_all = _memory_kv(mem, mem_norm_g, w_kv_bf)
    final_g = final_norm_g.reshape(1, d)
    for l in range(depth):
        x = _mixer(x, l, row(mix_norm_g), w_in_bf, conv_dw_w, row(conv_dw_b), row(conv_ln_g),
                   row(conv_ln_b), w_co_bf, w_pg_bf, row(pool_scale), w_out_bf)
        x = _xattn(x, l, row(xattn_norm_g), w_q_bf, kt_all, v_all, w_o_bf)
        x = _ffn(x, l, row(ffn_norm_g), w_up_bf, ffn_dw_w, w_dn_bf, final_g, l == depth - 1)
    return x
```

```python
import functools

import jax
import jax.numpy as jnp
from jax import lax
from jax.experimental import pallas as pl
from jax.experimental.pallas import tpu as pltpu

EPS = 1e-6
POOL_WINDOWS = (2, 4, 8, 16)
XA_HEADS = 4

F32 = jnp.float32
BF16 = jnp.bfloat16
U32 = jnp.uint32

LANES = 128
SUBLANES = 8
BF16_ROWS = 16
SEQ_TILE = 1024
FFN_SEQ_TILE = 512
NORM_ROWS = 256
CONV_ROWS = 128
KV_BATCH = 4
VMEM_LIMIT = 52 * 1024 * 1024


def _dot(a, b):
    return jnp.dot(a, b, preferred_element_type=F32)


def _rms(x, g):
    return x * lax.rsqrt(jnp.mean(x * x, axis=-1, keepdims=True) + EPS) * g


def _const_spec(shape, layer=None):
    n = len(shape)
    if layer is None:
        return pl.BlockSpec(shape, lambda b, s: (0,) * n, pipeline_mode=pl.Buffered(1))
    return pl.BlockSpec((None,) + shape, lambda b, s: (layer,) + (0,) * n,
                        pipeline_mode=pl.Buffered(1))


def _params():
    return pltpu.CompilerParams(dimension_semantics=("arbitrary", "arbitrary"),
                                vmem_limit_bytes=VMEM_LIMIT)


def _kv_kernel(mem_ref, g_ref, wkv_ref, kt_ref, v_ref):
    nb, m, d = mem_ref.shape
    mn = _rms(mem_ref[...].reshape(nb * m, d), g_ref[...]).astype(BF16)
    k = _dot(mn, wkv_ref[:, :d])
    v_ref[...] = _dot(mn, wkv_ref[:, d:]).astype(BF16).reshape(nb, m, d)
    for i in range(nb):
        kt_ref[i] = k[i * m:(i + 1) * m, :].T.astype(BF16)


def _memory_kv(mem, mem_norm_g, w_kv_bf):
    depth = w_kv_bf.shape[0]
    b, m, d = mem.shape
    nb = KV_BATCH if b % KV_BATCH == 0 else 1
    return pl.pallas_call(
        _kv_kernel,
        grid=(depth, b // nb),
        in_specs=[pl.BlockSpec((nb, m, d), lambda l, i: (i, 0, 0)),
                  pl.BlockSpec((1, d), lambda l, i: (0, 0)),
                  pl.BlockSpec((None, d, 2 * d), lambda l, i: (l, 0, 0))],
        out_specs=[pl.BlockSpec((None, nb, d, m), lambda l, i: (l, i, 0, 0)),
                   pl.BlockSpec((None, nb, m, d), lambda l, i: (l, i, 0, 0))],
        out_shape=[jax.ShapeDtypeStruct((depth, b, d, m), BF16),
                   jax.ShapeDtypeStruct((depth, b, m, d), BF16)],
        compiler_params=_params(),
        name="memory_kv",
    )(mem, mem_norm_g.reshape(1, d), w_kv_bf)


def _mixer_kernel(x_ref, g_ref, win_ref, dww_ref, dwb_ref, lng_ref, lnb_ref, wco_ref, wpg_ref,
                  ps_ref, wout_ref, o_ref, ybuf, ebuf, obuf, wbuf, tbuf, ubuf, cbuf, mbuf, hbuf,
                  *, hist):
    ts, d = x_ref.shape
    ncb = ybuf.shape[0]
    cc = ncb * LANES
    rows = hist + ts
    ngrp, gdim, gout = wpg_ref.shape
    cp = ngrp * gdim
    kc = dww_ref.shape[0]
    s = pl.program_id(1)

    @pl.when(s == 0)
    def _():
        ybuf[:, 0:hist, :] = jnp.zeros((ncb, hist, LANES), F32)
        ybuf[:, rows:rows + BF16_ROWS, :] = jnp.zeros((ncb, BF16_ROWS, LANES), F32)
        ubuf[:, 0:hist, :] = jnp.zeros((ngrp, hist, LANES), F32)
        for k in range(kc):
            wbuf[k] = jnp.broadcast_to(dww_ref[k:k + 1, :], (BF16_ROWS, cc)).astype(BF16)

    for r0 in range(0, ts, NORM_ROWS):
        rb = slice(r0, r0 + NORM_ROWS)
        hb = _rms(x_ref[rb, :], g_ref[...]).astype(BF16)
        hbuf[rb, :] = hb
        yc = _dot(hb, win_ref[:, 0:cc]) * jax.nn.sigmoid(_dot(hb, win_ref[:, cc:2 * cc]))
        for j in range(ncb):
            ybuf[j, hist + r0:hist + r0 + NORM_ROWS, :] = yc[:, j * LANES:(j + 1) * LANES]
    h = hbuf[...]
    u = _dot(h, win_ref[:, 2 * cc:2 * cc + cp])
    for j in range(ngrp):
        ubuf[j, hist:rows, :] = u[:, j * LANES:(j + 1) * LANES]

    for j in range(ncb):
        ebuf[j] = pltpu.bitcast(ybuf[j, 0:rows, :].astype(BF16), U32)
        obuf[j] = pltpu.bitcast(ybuf[j, 1:rows + 1, :].astype(BF16), U32)

    for j in range(ncb):
        lanes = slice(j * LANES, (j + 1) * LANES)
        taps = [jnp.concatenate([wbuf[k, :, lanes]] * (CONV_ROWS // BF16_ROWS), axis=0)
                for k in range(kc)]

        def conv_rows(n, carry, j=j, lanes=lanes, taps=taps):
            r0 = pl.multiple_of(n * CONV_ROWS, CONV_ROWS)
            w0 = pl.multiple_of(n * (CONV_ROWS // 2), CONV_ROWS // 2)
            acc = jnp.zeros((CONV_ROWS, LANES), F32)
            for k in range(kc):
                lo = hist - (kc - 1) + k
                src = ebuf if lo % 2 == 0 else obuf
                win = pltpu.bitcast(src[j, pl.ds(w0 + lo // 2, CONV_ROWS // 2), :], BF16)
                acc = acc + win.astype(F32) * taps[k].astype(F32)
            tbuf[pl.ds(r0, CONV_ROWS), lanes] = acc
            return carry

        lax.fori_loop(0, ts // CONV_ROWS, conv_rows, 0)

    yv = tbuf[...] + dwb_ref[...]
    xc = yv - jnp.mean(yv, axis=-1, keepdims=True)
    var = jnp.mean(xc * xc, axis=-1, keepdims=True)
    yn = xc * lax.rsqrt(var + EPS) * lng_ref[...] + lnb_ref[...]
    cbuf[...] = (yn * jax.nn.sigmoid(yn)).astype(BF16)

    t_glob = s * ts + lax.broadcasted_iota(jnp.int32, (ts, LANES), 0)
    g_conv0 = 2 * cc + cp
    g_pool0 = g_conv0 + d
    for j in range(ngrp):
        w = POOL_WINDOWS[j]
        cols = slice(j * gout, (j + 1) * gout)
        cur = ubuf[j, hist:rows, :]
        tot = cur
        for back in range(1, w):
            tot = tot + ubuf[j, hist - back:rows - back, :]
        cnt = jnp.minimum(t_glob + 1, w).astype(F32)
        zp = (tot / cnt - cur).astype(BF16)
        yp = _dot(zp, wpg_ref[j]) * ps_ref[:, cols]
        yo = _dot(cbuf[...], wco_ref[:, cols])
        gc = _dot(h, win_ref[:, g_conv0 + j * gout:g_conv0 + (j + 1) * gout])
        gp = _dot(h, win_ref[:, g_pool0 + j * gout:g_pool0 + (j + 1) * gout])
        mbuf[:, cols] = (jax.nn.sigmoid(gc) * yo + jax.nn.sigmoid(gp) * yp).astype(BF16)

    o_ref[...] = x_ref[...] + _dot(mbuf[...], wout_ref[...])

    ybuf[:, 0:hist, :] = ybuf[:, ts:rows, :]
    ubuf[:, 0:hist, :] = ubuf[:, ts:rows, :]


def _mixer(x, layer, g, w_in, dw_w, dw_b, ln_g, ln_b, w_co, w_pg, p_scale, w_out):
    b, s, d = x.shape
    ts = min(SEQ_TILE, s)
    n_in = w_in.shape[-1]
    kc, cc = dw_w.shape[1:]
    ngrp, gdim, gout = w_pg.shape[1:]
    cp = ngrp * gdim
    hist = -(-max(kc - 1, max(POOL_WINDOWS) - 1) // BF16_ROWS) * BF16_ROWS
    assert s % ts == 0 and ts % CONV_ROWS == 0 and CONV_ROWS % BF16_ROWS == 0 and hist <= ts
    assert ts % NORM_ROWS == 0
    assert cc % LANES == 0 and gdim == LANES
    assert n_in == 2 * cc + cp + 2 * d and ngrp == len(POOL_WINDOWS) and ngrp * gout == d
    xspec = pl.BlockSpec((None, ts, d), lambda i, j: (i, j, 0))
    return pl.pallas_call(
        functools.partial(_mixer_kernel, hist=hist),
        grid=(b, s // ts),
        in_specs=[xspec,
                  _const_spec((1, d), layer),
                  _const_spec((d, n_in), layer),
                  _const_spec((kc, cc), layer),
                  _const_spec((1, cc), layer),
                  _const_spec((1, cc), layer),
                  _const_spec((1, cc), layer),
                  _const_spec((cc, d), layer),
                  _const_spec((ngrp, gdim, gout), layer),
                  _const_spec((1, d), layer),
                  _const_spec((d, d), layer)],
        out_specs=xspec,
        out_shape=jax.ShapeDtypeStruct(x.shape, x.dtype),
        scratch_shapes=[pltpu.VMEM((cc // LANES, hist + ts + BF16_ROWS, LANES), F32),
                        pltpu.VMEM((cc // LANES, (hist + ts) // 2, LANES), U32),
                        pltpu.VMEM((cc // LANES, (hist + ts) // 2, LANES), U32),
                        pltpu.VMEM((kc, BF16_ROWS, cc), BF16),
                        pltpu.VMEM((ts, cc), F32),
                        pltpu.VMEM((ngrp, hist + ts, LANES), F32),
                        pltpu.VMEM((ts, cc), BF16),
                        pltpu.VMEM((ts, d), BF16),
                        pltpu.VMEM((ts, d), BF16)],
        compiler_params=_params(),
        name=f"mixer_l{layer}",
    )(x, g, w_in, dw_w, dw_b, ln_g, ln_b, w_co, w_pg, p_scale, w_out)


def _xattn_kernel(x_ref, g_ref, wq_ref, kt_ref, v_ref, wo_ref, o_ref, abuf, qbuf):
    ts, d = x_ref.shape
    hd = d // XA_HEADS
    for r0 in range(0, ts, NORM_ROWS):
        rows = slice(r0, r0 + NORM_ROWS)
        hq = _rms(x_ref[rows, :], g_ref[...]).astype(BF16)
        qbuf[rows, :] = (_dot(hq, wq_ref[...]) * (hd ** -0.5)).astype(BF16)
    heads = [slice(hh * hd, (hh + 1) * hd) for hh in range(XA_HEADS)]
    scores = [_dot(qbuf[:, cols], kt_ref[cols, :]) for cols in heads]
    for cols, sc in zip(heads, scores):
        p = jnp.exp(sc - jnp.max(sc, axis=-1, keepdims=True))
        l = jnp.sum(p, axis=-1, keepdims=True)
        abuf[:, cols] = (_dot(p.astype(BF16), v_ref[:, cols]) / l).astype(BF16)
    o_ref[...] = x_ref[...] + _dot(abuf[...], wo_ref[...])


def _xattn(x, layer, g, w_q, kt, v, w_o):
    b, s, d = x.shape
    m = v.shape[2]
    ts = min(SEQ_TILE, s)
    assert s % ts == 0 and ts % NORM_ROWS == 0 and d % XA_HEADS == 0
    xspec = pl.BlockSpec((None, ts, d), lambda i, j: (i, j, 0))
    return pl.pallas_call(
        _xattn_kernel,
        grid=(b, s // ts),
        in_specs=[xspec, _const_spec((1, d), layer), _const_spec((d, d), layer),
                  pl.BlockSpec((None, None, d, m), lambda i, j: (layer, i, 0, 0)),
                  pl.BlockSpec((None, None, m, d), lambda i, j: (layer, i, 0, 0)),
                  _const_spec((d, d), layer)],
        out_specs=xspec,
        out_shape=jax.ShapeDtypeStruct(x.shape, x.dtype),
        scratch_shapes=[pltpu.VMEM((ts, d), BF16), pltpu.VMEM((ts, d), BF16)],
        compiler_params=_params(),
        name=f"xattn_l{layer}",
    )(x, g, w_q, kt, v, w_o)


def _gelu_tanh(x):
    return 0.5 * x * (1.0 + jnp.tanh(0.7978845608028654 * (x + 0.044715 * (x * x * x))))


def _ffn_kernel(x_ref, g_ref, wup_ref, dw_ref, wdn_ref, fg_ref, o_ref, hbuf, ebuf, hist_ref,
                *, chunk, final_norm):
    ts, d = x_ref.shape
    dff = wdn_ref.shape[0]
    kf = dw_ref.shape[0]
    pad = SUBLANES
    s = pl.program_id(1)

    @pl.when(s == 0)
    def _():
        hist_ref[...] = jnp.zeros(hist_ref.shape, F32)

    x = x_ref[...]
    hf = _rms(x, g_ref[...]).astype(BF16)

    def conv_half(c0):
        ebuf[0:pad, :] = hist_ref[:, c0:c0 + chunk]
        ebuf[pad:pad + ts, :] = _dot(hf, wup_ref[:, c0:c0 + chunk])
        hist_ref[:, c0:c0 + chunk] = ebuf[ts:ts + pad, :]
        out = dw_ref[kf - 1:kf, c0:c0 + chunk] * ebuf[pad:pad + ts, :]
        for back in range(1, kf):
            out = out + dw_ref[kf - 1 - back:kf - back, c0:c0 + chunk] * ebuf[pad - back:pad - back + ts, :]
        return out

    for c in range(dff // chunk):
        gate = conv_half(c * chunk)
        val = conv_half(dff + c * chunk)
        hbuf[:, c * chunk:(c + 1) * chunk] = (_gelu_tanh(gate) * val).astype(BF16)

    y = x + _dot(hbuf[...], wdn_ref[...])
    if final_norm:
        y = _rms(y, fg_ref[...])
    o_ref[...] = y


def _ffn(x, layer, g, w_up, dw_w, w_down, final_g, final_norm):
    b, s, d = x.shape
    dff = w_down.shape[1]
    kf = dw_w.shape[1]
    ts = min(FFN_SEQ_TILE, s)
    chunk = 2 * LANES
    assert s % ts == 0 and dff % chunk == 0 and kf - 1 <= SUBLANES
    xspec = pl.BlockSpec((None, ts, d), lambda i, j: (i, j, 0))
    return pl.pallas_call(
        functools.partial(_ffn_kernel, chunk=chunk, final_norm=final_norm),
        grid=(b, s // ts),
        in_specs=[xspec, _const_spec((1, d), layer), _const_spec((d, 2 * dff), layer),
                  _const_spec((kf, 2 * dff), layer), _const_spec((dff, d), layer),
                  _const_spec((1, d))],
        out_specs=xspec,
        out_shape=jax.ShapeDtypeStruct(x.shape, x.dtype),
        scratch_shapes=[pltpu.VMEM((ts, dff), BF16),
                        pltpu.VMEM((SUBLANES + ts, chunk), F32),
                        pltpu.VMEM((SUBLANES, 2 * dff), F32)],
        compiler_params=_params(),
        name=f"ffn_l{layer}",
    )(x, g, w_up, dw_w, w_down, final_g)


def kernel(x, mem, mix_norm_g, w_in, conv_dw_w, conv_dw_b, conv_ln_g, conv_ln_b, w_conv_out,
           w_pool_grp, pool_scale, w_out, xattn_norm_g, mem_norm_g, w_q, w_kv, w_o,
           ffn_norm_g, w_up, ffn_dw_w, w_down, final_norm_g):
    depth = w_in.shape[0]
    d = x.shape[-1]
    row = lambda v: v.reshape(v.shape[0], 1, v.shape[1])
    w_in_bf, w_co_bf, w_pg_bf, w_out_bf = (w.astype(BF16) for w in (w_in, w_conv_out, w_pool_grp, w_out))
    w_q_bf, w_kv_bf, w_o_bf, w_up_bf, w_dn_bf = (w.astype(BF16) for w in (w_q, w_kv, w_o, w_up, w_down))
    kt_all, v_all = _memory_kv(mem, mem_norm_g, w_kv_bf)
    final_g = final_norm_g.reshape(1, d)
    for l in range(depth):
        x = _mixer(x, l, row(mix_norm_g), w_in_bf, conv_dw_w, row(conv_dw_b), row(conv_ln_g),
                   row(conv_ln_b), w_co_bf, w_pg_bf, row(pool_scale), w_out_bf)
        x = _xattn(x, l, row(xattn_norm_g), w_q_bf, kt_all, v_all, w_o_bf)
        x = _ffn(x, l, row(ffn_norm_g), w_up_bf, ffn_dw_w, w_dn_bf, final_g, l == depth - 1)
    return x
```

```python
import functools

import jax
import jax.numpy as jnp
from jax import lax
from jax.experimental import pallas as pl
from jax.experimental.pallas import tpu as pltpu

EPS = 1e-6
POOL_WINDOWS = (2, 4, 8, 16)
XA_HEADS = 4

F32 = jnp.float32
BF16 = jnp.bfloat16
U32 = jnp.uint32

LANES = 128
SUBLANES = 8
BF16_ROWS = 16
SEQ_TILE = 1024
FFN_SEQ_TILE = 512
NORM_ROWS = 256
CONV_ROWS = 128
KV_BATCH = 2
VMEM_LIMIT = 52 * 1024 * 1024


def _dot(a, b):
    return jnp.dot(a, b, preferred_element_type=F32)


def _rms(x, g):
    return x * lax.rsqrt(jnp.mean(x * x, axis=-1, keepdims=True) + EPS) * g


def _const_spec(shape, layer=None):
    n = len(shape)
    if layer is None:
        return pl.BlockSpec(shape, lambda b, s: (0,) * n, pipeline_mode=pl.Buffered(1))
    return pl.BlockSpec((None,) + shape, lambda b, s: (layer,) + (0,) * n,
                        pipeline_mode=pl.Buffered(1))


def _params():
    return pltpu.CompilerParams(dimension_semantics=("arbitrary", "arbitrary"),
                                vmem_limit_bytes=VMEM_LIMIT)


def _kv_kernel(mem_ref, g_ref, wkv_ref, *refs):
    n = (len(refs) - 3) // 2
    w_f32, (kt_ref, v_ref), w_bf, wkv_bf = refs[:n], refs[n:n + 2], refs[n + 2:2 * n + 2], refs[-1]
    nb, m, d = mem_ref.shape

    @pl.when(pl.program_id(1) == 0)
    def _():
        wkv_bf[...] = wkv_ref[...].astype(BF16)

    mn = _rms(mem_ref[...].reshape(nb * m, d), g_ref[...]).astype(BF16)
    k = _dot(mn, wkv_bf[:, :d])
    v_ref[...] = _dot(mn, wkv_bf[:, d:]).astype(BF16).reshape(nb, m, d)
    for i in range(nb):
        kt_ref[i] = k[i * m:(i + 1) * m, :].T.astype(BF16)
    for src, dst in zip(w_f32, w_bf):
        dst[...] = src[...].astype(BF16)


def _memory_kv_and_weight_casts(mem, mem_norm_g, w_kv, weights):
    depth = w_kv.shape[0]
    b, m, d = mem.shape
    nb = KV_BATCH if b % KV_BATCH == 0 else 1
    steps = b // nb
    slabs = [(w.shape[1] // steps, w.shape[2]) for w in weights]
    assert all(w.shape[1] % steps == 0 and r % BF16_ROWS == 0 for w, (r, _) in zip(weights, slabs))
    slab_specs = [pl.BlockSpec((None, r, c), lambda l, i: (l, i, 0)) for r, c in slabs]
    outs = pl.pallas_call(
        _kv_kernel,
        grid=(depth, steps),
        in_specs=[pl.BlockSpec((nb, m, d), lambda l, i: (i, 0, 0)),
                  pl.BlockSpec((1, d), lambda l, i: (0, 0)),
                  pl.BlockSpec((None, d, 2 * d), lambda l, i: (l, 0, 0),
                               pipeline_mode=pl.Buffered(1))] + slab_specs,
        out_specs=[pl.BlockSpec((None, nb, d, m), lambda l, i: (l, i, 0, 0)),
                   pl.BlockSpec((None, nb, m, d), lambda l, i: (l, i, 0, 0))] + slab_specs,
        out_shape=[jax.ShapeDtypeStruct((depth, b, d, m), BF16),
                   jax.ShapeDtypeStruct((depth, b, m, d), BF16)]
                  + [jax.ShapeDtypeStruct(w.shape, BF16) for w in weights],
        scratch_shapes=[pltpu.VMEM((d, 2 * d), BF16)],
        compiler_params=_params(),
        name="memory_kv",
    )(mem, mem_norm_g.reshape(1, d), w_kv, *weights)
    return outs[0], outs[1], outs[2:]


def _mixer_kernel(x_ref, g_ref, win_ref, dww_ref, dwb_ref, lng_ref, lnb_ref, wco_ref, wpg_ref,
                  ps_ref, wout_ref, o_ref, ybuf, ebuf, obuf, wbuf, tbuf, ubuf, cbuf, mbuf, hbuf,
                  *, hist):
    ts, d = x_ref.shape
    ncb = ybuf.shape[0]
    cc = ncb * LANES
    rows = hist + ts
    ngrp, gdim, gout = wpg_ref.shape
    cp = ngrp * gdim
    kc = dww_ref.shape[0]
    s = pl.program_id(1)

    @pl.when(s == 0)
    def _():
        ybuf[:, 0:hist, :] = jnp.zeros((ncb, hist, LANES), F32)
        ybuf[:, rows:rows + BF16_ROWS, :] = jnp.zeros((ncb, BF16_ROWS, LANES), F32)
        ubuf[:, 0:hist, :] = jnp.zeros((ngrp, hist, LANES), F32)
        for k in range(kc):
            wbuf[k] = jnp.broadcast_to(dww_ref[k:k + 1, :], (BF16_ROWS, cc)).astype(BF16)

    for r0 in range(0, ts, NORM_ROWS):
        rb = slice(r0, r0 + NORM_ROWS)
        hb = _rms(x_ref[rb, :], g_ref[...]).astype(BF16)
        hbuf[rb, :] = hb
        yc = _dot(hb, win_ref[:, 0:cc]) * jax.nn.sigmoid(_dot(hb, win_ref[:, cc:2 * cc]))
        for j in range(ncb):
            ybuf[j, hist + r0:hist + r0 + NORM_ROWS, :] = yc[:, j * LANES:(j + 1) * LANES]
    h = hbuf[...]
    u = _dot(h, win_ref[:, 2 * cc:2 * cc + cp])
    for j in range(ngrp):
        ubuf[j, hist:rows, :] = u[:, j * LANES:(j + 1) * LANES]

    for j in range(ncb):
        ebuf[j] = pltpu.bitcast(ybuf[j, 0:rows, :].astype(BF16), U32)
        obuf[j] = pltpu.bitcast(ybuf[j, 1:rows + 1, :].astype(BF16), U32)

    for j in range(ncb):
        lanes = slice(j * LANES, (j + 1) * LANES)
        taps = [jnp.concatenate([wbuf[k, :, lanes]] * (CONV_ROWS // BF16_ROWS), axis=0)
                for k in range(kc)]

        def conv_rows(n, carry, j=j, lanes=lanes, taps=taps):
            r0 = pl.multiple_of(n * CONV_ROWS, CONV_ROWS)
            w0 = pl.multiple_of(n * (CONV_ROWS // 2), CONV_ROWS // 2)
            acc = jnp.zeros((CONV_ROWS, LANES), F32)
            for k in range(kc):
                lo = hist - (kc - 1) + k
                src = ebuf if lo % 2 == 0 else obuf
                win = pltpu.bitcast(src[j, pl.ds(w0 + lo // 2, CONV_ROWS // 2), :], BF16)
                acc = acc + win.astype(F32) * taps[k].astype(F32)
            tbuf[pl.ds(r0, CONV_ROWS), lanes] = acc
            return carry

        lax.fori_loop(0, ts // CONV_ROWS, conv_rows, 0)

    yv = tbuf[...] + dwb_ref[...]
    xc = yv - jnp.mean(yv, axis=-1, keepdims=True)
    var = jnp.mean(xc * xc, axis=-1, keepdims=True)
    yn = xc * lax.rsqrt(var + EPS) * lng_ref[...] + lnb_ref[...]
    cbuf[...] = (yn * jax.nn.sigmoid(yn)).astype(BF16)

    t_glob = s * ts + lax.broadcasted_iota(jnp.int32, (ts, LANES), 0)
    g_conv0 = 2 * cc + cp
    g_pool0 = g_conv0 + d
    for j in range(ngrp):
        w = POOL_WINDOWS[j]
        cols = slice(j * gout, (j + 1) * gout)
        cur = ubuf[j, hist:rows, :]
        tot = cur
        for back in range(1, w):
            tot = tot + ubuf[j, hist - back:rows - back, :]
        cnt = jnp.minimum(t_glob + 1, w).astype(F32)
        zp = (tot / cnt - cur).astype(BF16)
        yp = _dot(zp, wpg_ref[j]) * ps_ref[:, cols]
        yo = _dot(cbuf[...], wco_ref[:, cols])
        gc = _dot(h, win_ref[:, g_conv0 + j * gout:g_conv0 + (j + 1) * gout])
        gp = _dot(h, win_ref[:, g_pool0 + j * gout:g_pool0 + (j + 1) * gout])
        mbuf[:, cols] = (jax.nn.sigmoid(gc) * yo + jax.nn.sigmoid(gp) * yp).astype(BF16)

    o_ref[...] = x_ref[...] + _dot(mbuf[...], wout_ref[...])

    ybuf[:, 0:hist, :] = ybuf[:, ts:rows, :]
    ubuf[:, 0:hist, :] = ubuf[:, ts:rows, :]


def _mixer(x, layer, g, w_in, dw_w, dw_b, ln_g, ln_b, w_co, w_pg, p_scale, w_out):
    b, s, d = x.shape
    ts = min(SEQ_TILE, s)
    n_in = w_in.shape[-1]
    kc, cc = dw_w.shape[1:]
    ngrp, gdim, gout = w_pg.shape[1:]
    cp = ngrp * gdim
    hist = -(-max(kc - 1, max(POOL_WINDOWS) - 1) // BF16_ROWS) * BF16_ROWS
    assert s % ts == 0 and ts % CONV_ROWS == 0 and CONV_ROWS % BF16_ROWS == 0 and hist <= ts
    assert ts % NORM_ROWS == 0
    assert cc % LANES == 0 and gdim == LANES
    assert n_in == 2 * cc + cp + 2 * d and ngrp == len(POOL_WINDOWS) and ngrp * gout == d
    xspec = pl.BlockSpec((None, ts, d), lambda i, j: (i, j, 0))
    return pl.pallas_call(
        functools.partial(_mixer_kernel, hist=hist),
        grid=(b, s // ts),
        in_specs=[xspec,
                  _const_spec((1, d), layer),
                  _const_spec((d, n_in), layer),
                  _const_spec((kc, cc), layer),
                  _const_spec((1, cc), layer),
                  _const_spec((1, cc), layer),
                  _const_spec((1, cc), layer),
                  _const_spec((cc, d), layer),
                  _const_spec((ngrp, gdim, gout), layer),
                  _const_spec((1, d), layer),
                  _const_spec((d, d), layer)],
        out_specs=xspec,
        out_shape=jax.ShapeDtypeStruct(x.shape, x.dtype),
        scratch_shapes=[pltpu.VMEM((cc // LANES, hist + ts + BF16_ROWS, LANES), F32),
                        pltpu.VMEM((cc // LANES, (hist + ts) // 2, LANES), U32),
                        pltpu.VMEM((cc // LANES, (hist + ts) // 2, LANES), U32),
                        pltpu.VMEM((kc, BF16_ROWS, cc), BF16),
                        pltpu.VMEM((ts, cc), F32),
                        pltpu.VMEM((ngrp, hist + ts, LANES), F32),
                        pltpu.VMEM((ts, cc), BF16),
                        pltpu.VMEM((ts, d), BF16),
                        pltpu.VMEM((ts, d), BF16)],
        compiler_params=_params(),
        name=f"mixer_l{layer}",
    )(x, g, w_in, dw_w, dw_b, ln_g, ln_b, w_co, w_pg, p_scale, w_out)


def _xattn_kernel(x_ref, g_ref, wq_ref, kt_ref, v_ref, wo_ref, *refs):
    n = (len(refs) - 3) // 2
    w_f32, o_ref, w_bf, (abuf, qbuf) = refs[:n], refs[n], refs[n + 1:2 * n + 1], refs[-2:]
    for src, dst in zip(w_f32, w_bf):
        dst[...] = src[...].astype(BF16)
    ts, d = x_ref.shape
    hd = d // XA_HEADS
    for r0 in range(0, ts, NORM_ROWS):
        rows = slice(r0, r0 + NORM_ROWS)
        hq = _rms(x_ref[rows, :], g_ref[...]).astype(BF16)
        qbuf[rows, :] = (_dot(hq, wq_ref[...]) * (hd ** -0.5)).astype(BF16)
    heads = [slice(hh * hd, (hh + 1) * hd) for hh in range(XA_HEADS)]
    scores = [_dot(qbuf[:, cols], kt_ref[cols, :]) for cols in heads]
    for cols, sc in zip(heads, scores):
        p = jnp.exp(sc - jnp.max(sc, axis=-1, keepdims=True))
        l = jnp.sum(p, axis=-1, keepdims=True)
        abuf[:, cols] = (_dot(p.astype(BF16), v_ref[:, cols]) / l).astype(BF16)
    o_ref[...] = x_ref[...] + _dot(abuf[...], wo_ref[...])


def _xattn(x, layer, g, w_q, kt, v, w_o, cast_weights=()):
    b, s, d = x.shape
    m = v.shape[2]
    ts = min(SEQ_TILE, s)
    ns = s // ts
    assert s % ts == 0 and ts % NORM_ROWS == 0 and d % XA_HEADS == 0
    slabs = [(w.shape[0], w.shape[1] // (b * ns), w.shape[2]) for w in cast_weights]
    assert all(w.shape[1] % (b * ns) == 0 and r % BF16_ROWS == 0 for w, (_, r, _) in zip(cast_weights, slabs))
    slab_specs = [pl.BlockSpec(blk, lambda i, j: (0, i * ns + j, 0)) for blk in slabs]
    xspec = pl.BlockSpec((None, ts, d), lambda i, j: (i, j, 0))
    outs = pl.pallas_call(
        _xattn_kernel,
        grid=(b, ns),
        in_specs=[xspec, _const_spec((1, d), layer), _const_spec((d, d), layer),
                  pl.BlockSpec((None, None, d, m), lambda i, j: (layer, i, 0, 0)),
                  pl.BlockSpec((None, None, m, d), lambda i, j: (layer, i, 0, 0)),
                  _const_spec((d, d), layer)] + slab_specs,
        out_specs=[xspec] + slab_specs,
        out_shape=[jax.ShapeDtypeStruct(x.shape, x.dtype)]
                  + [jax.ShapeDtypeStruct(w.shape, BF16) for w in cast_weights],
        scratch_shapes=[pltpu.VMEM((ts, d), BF16), pltpu.VMEM((ts, d), BF16)],
        compiler_params=_params(),
        name=f"xattn_l{layer}",
    )(x, g, w_q, kt, v, w_o, *cast_weights)
    return outs[0], outs[1:]


def _gelu_tanh(x):
    return 0.5 * x * (1.0 + jnp.tanh(0.7978845608028654 * (x + 0.044715 * (x * x * x))))


def _ffn_kernel(x_ref, g_ref, wup_ref, dw_ref, wdn_ref, fg_ref, o_ref, hbuf, ebuf, hist_ref,
                *, chunk, final_norm):
    ts, d = x_ref.shape
    dff = wdn_ref.shape[0]
    kf = dw_ref.shape[0]
    pad = SUBLANES
    s = pl.program_id(1)

    @pl.when(s == 0)
    def _():
        hist_ref[...] = jnp.zeros(hist_ref.shape, F32)

    x = x_ref[...]
    hf = _rms(x, g_ref[...]).astype(BF16)

    def conv_half(c0):
        ebuf[0:pad, :] = hist_ref[:, c0:c0 + chunk]
        ebuf[pad:pad + ts, :] = _dot(hf, wup_ref[:, c0:c0 + chunk])
        hist_ref[:, c0:c0 + chunk] = ebuf[ts:ts + pad, :]
        out = dw_ref[kf - 1:kf, c0:c0 + chunk] * ebuf[pad:pad + ts, :]
        for back in range(1, kf):
            out = out + dw_ref[kf - 1 - back:kf - back, c0:c0 + chunk] * ebuf[pad - back:pad - back + ts, :]
        return out

    for c in range(dff // chunk):
        gate = conv_half(c * chunk)
        val = conv_half(dff + c * chunk)
        hbuf[:, c * chunk:(c + 1) * chunk] = (_gelu_tanh(gate) * val).astype(BF16)

    y = x + _dot(hbuf[...], wdn_ref[...])
    if final_norm:
        y = _rms(y, fg_ref[...])
    o_ref[...] = y


def _ffn(x, layer, g, w_up, dw_w, w_down, final_g, final_norm):
    b, s, d = x.shape
    dff = w_down.shape[1]
    kf = dw_w.shape[1]
    ts = min(FFN_SEQ_TILE, s)
    chunk = 2 * LANES
    assert s % ts == 0 and dff % chunk == 0 and kf - 1 <= SUBLANES
    xspec = pl.BlockSpec((None, ts, d), lambda i, j: (i, j, 0))
    return pl.pallas_call(
        functools.partial(_ffn_kernel, chunk=chunk, final_norm=final_norm),
        grid=(b, s // ts),
        in_specs=[xspec, _const_spec((1, d), layer), _const_spec((d, 2 * dff), layer),
                  _const_spec((kf, 2 * dff), layer), _const_spec((dff, d), layer),
                  _const_spec((1, d))],
        out_specs=xspec,
        out_shape=jax.ShapeDtypeStruct(x.shape, x.dtype),
        scratch_shapes=[pltpu.VMEM((ts, dff), BF16),
                        pltpu.VMEM((SUBLANES + ts, chunk), F32),
                        pltpu.VMEM((SUBLANES, 2 * dff), F32)],
        compiler_params=_params(),
        name=f"ffn_l{layer}",
    )(x, g, w_up, dw_w, w_down, final_g)


def kernel(x, mem, mix_norm_g, w_in, conv_dw_w, conv_dw_b, conv_ln_g, conv_ln_b, w_conv_out,
           w_pool_grp, pool_scale, w_out, xattn_norm_g, mem_norm_g, w_q, w_kv, w_o,
           ffn_norm_g, w_up, ffn_dw_w, w_down, final_norm_g):
    depth = w_in.shape[0]
    d = x.shape[-1]
    row = lambda v: v.reshape(v.shape[0], 1, v.shape[1])
    pg_shape = w_pool_grp.shape
    w_pg_rows = w_pool_grp.reshape(depth, pg_shape[1] * pg_shape[2], pg_shape[3])
    kt_all, v_all, (w_in_bf, w_co_bf, w_pg_bf, w_out_bf, w_q_bf, w_o_bf) = (
        _memory_kv_and_weight_casts(mem, mem_norm_g, w_kv,
                                    (w_in, w_conv_out, w_pg_rows, w_out, w_q, w_o)))
    w_pg_bf = w_pg_bf.reshape(pg_shape)
    final_g = final_norm_g.reshape(1, d)
    dn_shape = w_down.shape
    ffn_f32 = (w_up, w_down.reshape(depth, d, dn_shape[1] * dn_shape[2] // d))
    for l in range(depth):
        x = _mixer(x, l, row(mix_norm_g), w_in_bf, conv_dw_w, row(conv_dw_b), row(conv_ln_g),
                   row(conv_ln_b), w_co_bf, w_pg_bf, row(pool_scale), w_out_bf)
        x, cast = _xattn(x, l, row(xattn_norm_g), w_q_bf, kt_all, v_all, w_o_bf,
                         ffn_f32 if l == 0 else ())
        if l == 0:
            w_up_bf, w_dn_bf = cast[0], cast[1].reshape(dn_shape)
        x = _ffn(x, l, row(ffn_norm_g), w_up_bf, ffn_dw_w, w_dn_bf, final_g, l == depth - 1)
    return x
```

```python
import functools

import jax
import jax.numpy as jnp
from jax import lax
from jax.experimental import pallas as pl
from jax.experimental.pallas import tpu as pltpu

EPS = 1e-6
POOL_WINDOWS = (2, 4, 8, 16)
XA_HEADS = 4

F32 = jnp.float32
BF16 = jnp.bfloat16
U32 = jnp.uint32

LANES = 128
SUBLANES = 8
BF16_ROWS = 16
SEQ_TILE = 1024
FFN_SEQ_TILE = 512
NORM_ROWS = 256
CONV_ROWS = 128
KV_BATCH = 2
VMEM_LIMIT = 52 * 1024 * 1024


def _dot(a, b):
    return jnp.dot(a, b, preferred_element_type=F32)


def _rms(x, g):
    return x * lax.rsqrt(jnp.mean(x * x, axis=-1, keepdims=True) + EPS) * g


def _const_spec(shape, layer=None):
    n = len(shape)
    if layer is None:
        return pl.BlockSpec(shape, lambda b, s: (0,) * n, pipeline_mode=pl.Buffered(1))
    return pl.BlockSpec((None,) + shape, lambda b, s: (layer,) + (0,) * n,
                        pipeline_mode=pl.Buffered(1))


def _params():
    return pltpu.CompilerParams(dimension_semantics=("arbitrary", "arbitrary"),
                                vmem_limit_bytes=VMEM_LIMIT)


def _kv_kernel(mem_ref, g_ref, wkv_ref, *refs):
    n = (len(refs) - 3) // 2
    w_f32, (kt_ref, v_ref), w_bf, wkv_bf = refs[:n], refs[n:n + 2], refs[n + 2:2 * n + 2], refs[-1]
    nb, m, d = mem_ref.shape

    @pl.when(pl.program_id(1) == 0)
    def _():
        wkv_bf[...] = wkv_ref[...].astype(BF16)

    mn = _rms(mem_ref[...].reshape(nb * m, d), g_ref[...]).astype(BF16)
    k = _dot(mn, wkv_bf[:, :d])
    v_ref[...] = _dot(mn, wkv_bf[:, d:]).astype(BF16).reshape(nb, m, d)
    for i in range(nb):
        kt_ref[i] = k[i * m:(i + 1) * m, :].T.astype(BF16)
    for src, dst in zip(w_f32, w_bf):
        dst[...] = src[...].astype(BF16)


def _memory_kv_and_weight_casts(mem, mem_norm_g, w_kv, weights):
    depth = w_kv.shape[0]
    b, m, d = mem.shape
    nb = KV_BATCH if b % KV_BATCH == 0 else 1
    steps = b // nb
    slabs = [(w.shape[1] // steps, w.shape[2]) for w in weights]
    assert all(w.shape[1] % steps == 0 and r % BF16_ROWS == 0 for w, (r, _) in zip(weights, slabs))
    slab_specs = [pl.BlockSpec((None, r, c), lambda l, i: (l, i, 0)) for r, c in slabs]
    outs = pl.pallas_call(
        _kv_kernel,
        grid=(depth, steps),
        in_specs=[pl.BlockSpec((nb, m, d), lambda l, i: (i, 0, 0)),
                  pl.BlockSpec((1, d), lambda l, i: (0, 0)),
                  pl.BlockSpec((None, d, 2 * d), lambda l, i: (l, 0, 0),
                               pipeline_mode=pl.Buffered(1))] + slab_specs,
        out_specs=[pl.BlockSpec((None, nb, d, m), lambda l, i: (l, i, 0, 0)),
                   pl.BlockSpec((None, nb, m, d), lambda l, i: (l, i, 0, 0))] + slab_specs,
        out_shape=[jax.ShapeDtypeStruct((depth, b, d, m), BF16),
                   jax.ShapeDtypeStruct((depth, b, m, d), BF16)]
                  + [jax.ShapeDtypeStruct(w.shape, BF16) for w in weights],
        scratch_shapes=[pltpu.VMEM((d, 2 * d), BF16)],
        compiler_params=_params(),
        name="memory_kv",
    )(mem, mem_norm_g.reshape(1, d), w_kv, *weights)
    return outs[0], outs[1], outs[2:]


def _mixer_kernel(x_ref, g_ref, win_ref, dww_ref, dwb_ref, lng_ref, lnb_ref, wco_ref, wpg_ref,
                  ps_ref, wout_ref, o_ref, ybuf, ebuf, obuf, wbuf, tbuf, ubuf, cbuf, mbuf, hbuf,
                  *, hist):
    ts, d = x_ref.shape
    ncb = ybuf.shape[0]
    cc = ncb * LANES
    rows = hist + ts
    ngrp, gdim, gout = wpg_ref.shape
    cp = ngrp * gdim
    kc = dww_ref.shape[0]
    s = pl.program_id(1)

    @pl.when(s == 0)
    def _():
        ybuf[:, 0:hist, :] = jnp.zeros((ncb, hist, LANES), F32)
        ybuf[:, rows:rows + BF16_ROWS, :] = jnp.zeros((ncb, BF16_ROWS, LANES), F32)
        ubuf[:, 0:hist, :] = jnp.zeros((ngrp, hist, LANES), F32)
        for k in range(kc):
            wbuf[k] = jnp.broadcast_to(dww_ref[k:k + 1, :], (BF16_ROWS, cc)).astype(BF16)

    for r0 in range(0, ts, NORM_ROWS):
        rb = slice(r0, r0 + NORM_ROWS)
        hb = _rms(x_ref[rb, :], g_ref[...]).astype(BF16)
        hbuf[rb, :] = hb
        yc = _dot(hb, win_ref[:, 0:cc]) * jax.nn.sigmoid(_dot(hb, win_ref[:, cc:2 * cc]))
        for j in range(ncb):
            ybuf[j, hist + r0:hist + r0 + NORM_ROWS, :] = yc[:, j * LANES:(j + 1) * LANES]
    h = hbuf[...]
    u = _dot(h, win_ref[:, 2 * cc:2 * cc + cp])
    for j in range(ngrp):
        ubuf[j, hist:rows, :] = u[:, j * LANES:(j + 1) * LANES]

    for j in range(ncb):
        ebuf[j] = pltpu.bitcast(ybuf[j, 0:rows, :].astype(BF16), U32)
        obuf[j] = pltpu.bitcast(ybuf[j, 1:rows + 1, :].astype(BF16), U32)

    for j in range(ncb):
        lanes = slice(j * LANES, (j + 1) * LANES)
        taps = [jnp.concatenate([wbuf[k, :, lanes]] * (CONV_ROWS // BF16_ROWS), axis=0)
                for k in range(kc)]

        def conv_rows(n, carry, j=j, lanes=lanes, taps=taps):
            r0 = pl.multiple_of(n * CONV_ROWS, CONV_ROWS)
            w0 = pl.multiple_of(n * (CONV_ROWS // 2), CONV_ROWS // 2)
            acc = jnp.zeros((CONV_ROWS, LANES), F32)
            for k in range(kc):
                lo = hist - (kc - 1) + k
                src = ebuf if lo % 2 == 0 else obuf
                win = pltpu.bitcast(src[j, pl.ds(w0 + lo // 2, CONV_ROWS // 2), :], BF16)
                acc = acc + win.astype(F32) * taps[k].astype(F32)
            tbuf[pl.ds(r0, CONV_ROWS), lanes] = acc
            return carry

        lax.fori_loop(0, ts // CONV_ROWS, conv_rows, 0)

    yv = tbuf[...] + dwb_ref[...]
    xc = yv - jnp.mean(yv, axis=-1, keepdims=True)
    var = jnp.mean(xc * xc, axis=-1, keepdims=True)
    yn = xc * lax.rsqrt(var + EPS) * lng_ref[...] + lnb_ref[...]
    cbuf[...] = (yn * jax.nn.sigmoid(yn)).astype(BF16)

    t_glob = s * ts + lax.broadcasted_iota(jnp.int32, (ts, LANES), 0)
    g_conv0 = 2 * cc + cp
    g_pool0 = g_conv0 + d
    for j in range(ngrp):
        w = POOL_WINDOWS[j]
        cols = slice(j * gout, (j + 1) * gout)
        cur = ubuf[j, hist:rows, :]
        tot = cur
        for back in range(1, w):
            tot = tot + ubuf[j, hist - back:rows - back, :]
        cnt = jnp.minimum(t_glob + 1, w).astype(F32)
        zp = (tot / cnt - cur).astype(BF16)
        yp = _dot(zp, wpg_ref[j]) * ps_ref[:, cols]
        yo = _dot(cbuf[...], wco_ref[:, cols])
        gc = _dot(h, win_ref[:, g_conv0 + j * gout:g_conv0 + (j + 1) * gout])
        gp = _dot(h, win_ref[:, g_pool0 + j * gout:g_pool0 + (j + 1) * gout])
        mbuf[:, cols] = (jax.nn.sigmoid(gc) * yo + jax.nn.sigmoid(gp) * yp).astype(BF16)

    o_ref[...] = x_ref[...] + _dot(mbuf[...], wout_ref[...])

    ybuf[:, 0:hist, :] = ybuf[:, ts:rows, :]
    ubuf[:, 0:hist, :] = ubuf[:, ts:rows, :]


def _mixer(x, layer, g, w_in, dw_w, dw_b, ln_g, ln_b, w_co, w_pg, p_scale, w_out):
    b, s, d = x.shape
    ts = min(SEQ_TILE, s)
    n_in = w_in.shape[-1]
    kc, cc = dw_w.shape[1:]
    ngrp, gdim, gout = w_pg.shape[1:]
    cp = ngrp * gdim
    hist = -(-max(kc - 1, max(POOL_WINDOWS) - 1) // BF16_ROWS) * BF16_ROWS
    assert s % ts == 0 and ts % CONV_ROWS == 0 and CONV_ROWS % BF16_ROWS == 0 and hist <= ts
    assert ts % NORM_ROWS == 0
    assert cc % LANES == 0 and gdim == LANES
    assert n_in == 2 * cc + cp + 2 * d and ngrp == len(POOL_WINDOWS) and ngrp * gout == d
    xspec = pl.BlockSpec((None, ts, d), lambda i, j: (i, j, 0))
    return pl.pallas_call(
        functools.partial(_mixer_kernel, hist=hist),
        grid=(b, s // ts),
        in_specs=[xspec,
                  _const_spec((1, d), layer),
                  _const_spec((d, n_in), layer),
                  _const_spec((kc, cc), layer),
                  _const_spec((1, cc), layer),
                  _const_spec((1, cc), layer),
                  _const_spec((1, cc), layer),
                  _const_spec((cc, d), layer),
                  _const_spec((ngrp, gdim, gout), layer),
                  _const_spec((1, d), layer),
                  _const_spec((d, d), layer)],
        out_specs=xspec,
        out_shape=jax.ShapeDtypeStruct(x.shape, x.dtype),
        scratch_shapes=[pltpu.VMEM((cc // LANES, hist + ts + BF16_ROWS, LANES), F32),
                        pltpu.VMEM((cc // LANES, (hist + ts) // 2, LANES), U32),
                        pltpu.VMEM((cc // LANES, (hist + ts) // 2, LANES), U32),
                        pltpu.VMEM((kc, BF16_ROWS, cc), BF16),
                        pltpu.VMEM((ts, cc), F32),
                        pltpu.VMEM((ngrp, hist + ts, LANES), F32),
                        pltpu.VMEM((ts, cc), BF16),
                        pltpu.VMEM((ts, d), BF16),
                        pltpu.VMEM((ts, d), BF16)],
        compiler_params=_params(),
        name=f"mixer_l{layer}",
    )(x, g, w_in, dw_w, dw_b, ln_g, ln_b, w_co, w_pg, p_scale, w_out)


def _xattn_kernel(x_ref, g_ref, wq_ref, kt_ref, v_ref, wo_ref, *refs):
    n = (len(refs) - 3) // 2
    w_f32, o_ref, w_bf, (abuf, qbuf) = refs[:n], refs[n], refs[n + 1:2 * n + 1], refs[-2:]
    for src, dst in zip(w_f32, w_bf):
        dst[...] = src[...].astype(BF16)
    ts, d = x_ref.shape
    hd = d // XA_HEADS
    for r0 in range(0, ts, NORM_ROWS):
        rows = slice(r0, r0 + NORM_ROWS)
        hq = _rms(x_ref[rows, :], g_ref[...]).astype(BF16)
        qbuf[rows, :] = (_dot(hq, wq_ref[...]) * (hd ** -0.5)).astype(BF16)
    heads = [slice(hh * hd, (hh + 1) * hd) for hh in range(XA_HEADS)]
    scores = [_dot(qbuf[:, cols], kt_ref[cols, :]) for cols in heads]
    for cols, sc in zip(heads, scores):
        p = jnp.exp(sc - jnp.max(sc, axis=-1, keepdims=True))
        l = jnp.sum(p, axis=-1, keepdims=True)
        abuf[:, cols] = (_dot(p.astype(BF16), v_ref[:, cols]) / l).astype(BF16)
    o_ref[...] = x_ref[...] + _dot(abuf[...], wo_ref[...])


def _xattn(x, layer, g, w_q, kt, v, w_o, cast_weights=()):
    b, s, d = x.shape
    m = v.shape[2]
    ts = min(SEQ_TILE, s)
    ns = s // ts
    assert s % ts == 0 and ts % NORM_ROWS == 0 and d % XA_HEADS == 0
    slabs = [(w.shape[0], w.shape[1] // (b * ns), w.shape[2]) for w in cast_weights]
    assert all(w.shape[1] % (b * ns) == 0 and r % BF16_ROWS == 0 for w, (_, r, _) in zip(cast_weights, slabs))
    slab_specs = [pl.BlockSpec(blk, lambda i, j: (0, i * ns + j, 0)) for blk in slabs]
    xspec = pl.BlockSpec((None, ts, d), lambda i, j: (i, j, 0))
    outs = pl.pallas_call(
        _xattn_kernel,
        grid=(b, ns),
        in_specs=[xspec, _const_spec((1, d), layer), _const_spec((d, d), layer),
                  pl.BlockSpec((None, None, d, m), lambda i, j: (layer, i, 0, 0)),
                  pl.BlockSpec((None, None, m, d), lambda i, j: (layer, i, 0, 0)),
                  _const_spec((d, d), layer)] + slab_specs,
        out_specs=[xspec] + slab_specs,
        out_shape=[jax.ShapeDtypeStruct(x.shape, x.dtype)]
                  + [jax.ShapeDtypeStruct(w.shape, BF16) for w in cast_weights],
        scratch_shapes=[pltpu.VMEM((ts, d), BF16), pltpu.VMEM((ts, d), BF16)],
        compiler_params=_params(),
        name=f"xattn_l{layer}",
    )(x, g, w_q, kt, v, w_o, *cast_weights)
    return outs[0], outs[1:]


def _gelu_tanh(x):
    return 0.5 * x * (1.0 + jnp.tanh(0.7978845608028654 * (x + 0.044715 * (x * x * x))))


def _ffn_kernel(x_ref, g_ref, wup_ref, dw_ref, wdn_ref, fg_ref, o_ref, hbuf, ebuf, hist_ref,
                *, chunk, final_norm):
    ts, d = x_ref.shape
    dff = wdn_ref.shape[0]
    kf = dw_ref.shape[0]
    pad = SUBLANES
    s = pl.program_id(1)

    @pl.when(s == 0)
    def _():
        hist_ref[...] = jnp.zeros(hist_ref.shape, F32)

    x = x_ref[...]
    hf = _rms(x, g_ref[...]).astype(BF16)

    def conv_half(c0):
        ebuf[0:pad, :] = hist_ref[:, c0:c0 + chunk]
        ebuf[pad:pad + ts, :] = _dot(hf, wup_ref[:, c0:c0 + chunk])
        hist_ref[:, c0:c0 + chunk] = ebuf[ts:ts + pad, :]
        out = dw_ref[kf - 1:kf, c0:c0 + chunk] * ebuf[pad:pad + ts, :]
        for back in range(1, kf):
            out = out + dw_ref[kf - 1 - back:kf - back, c0:c0 + chunk] * ebuf[pad - back:pad - back + ts, :]
        return out

    for c in range(dff // chunk):
        gate = conv_half(c * chunk)
        val = conv_half(dff + c * chunk)
        hbuf[:, c * chunk:(c + 1) * chunk] = (_gelu_tanh(gate) * val).astype(BF16)

    y = x + _dot(hbuf[...], wdn_ref[...])
    if final_norm:
        y = _rms(y, fg_ref[...])
    o_ref[...] = y


def _ffn(x, layer, g, w_up, dw_w, w_down, final_g, final_norm):
    b, s, d = x.shape
    dff = w_down.shape[1]
    kf = dw_w.shape[1]
    ts = min(FFN_SEQ_TILE, s)
    chunk = 2 * LANES
    assert s % ts == 0 and dff % chunk == 0 and kf - 1 <= SUBLANES
    xspec = pl.BlockSpec((None, ts, d), lambda i, j: (i, j, 0))
    return pl.pallas_call(
        functools.partial(_ffn_kernel, chunk=chunk, final_norm=final_norm),
        grid=(b, s // ts),
        in_specs=[xspec, _const_spec((1, d), layer), _const_spec((d, 2 * dff), layer),
                  _const_spec((kf, 2 * dff), layer), _const_spec((dff, d), layer),
                  _const_spec((1, d))],
        out_specs=xspec,
        out_shape=jax.ShapeDtypeStruct(x.shape, x.dtype),
        scratch_shapes=[pltpu.VMEM((ts, dff), BF16),
                        pltpu.VMEM((SUBLANES + ts, chunk), F32),
                        pltpu.VMEM((SUBLANES, 2 * dff), F32)],
        compiler_params=_params(),
        name=f"ffn_l{layer}",
    )(x, g, w_up, dw_w, w_down, final_g)


def kernel(x, mem, mix_norm_g, w_in, conv_dw_w, conv_dw_b, conv_ln_g, conv_ln_b, w_conv_out,
           w_pool_grp, pool_scale, w_out, xattn_norm_g, mem_norm_g, w_q, w_kv, w_o,
           ffn_norm_g, w_up, ffn_dw_w, w_down, final_norm_g):
    depth = w_in.shape[0]
    d = x.shape[-1]
    row = lambda v: v.reshape(v.shape[0], 1, v.shape[1])
    pg_shape = w_pool_grp.shape
    w_pg_rows = w_pool_grp.reshape(depth, pg_shape[1] * pg_shape[2], pg_shape[3])
    kt_all, v_all, (w_in_bf, w_co_bf, w_pg_bf, w_out_bf, w_q_bf, w_o_bf, w_dn_bf) = (
        _memory_kv_and_weight_casts(mem, mem_norm_g, w_kv,
                                    (w_in, w_conv_out, w_pg_rows, w_out, w_q, w_o, w_down)))
    w_pg_bf = w_pg_bf.reshape(pg_shape)
    final_g = final_norm_g.reshape(1, d)
    for l in range(depth):
        x = _mixer(x, l, row(mix_norm_g), w_in_bf, conv_dw_w, row(conv_dw_b), row(conv_ln_g),
                   row(conv_ln_b), w_co_bf, w_pg_bf, row(pool_scale), w_out_bf)
        x, cast = _xattn(x, l, row(xattn_norm_g), w_q_bf, kt_all, v_all, w_o_bf,
                         (w_up,) if l == 0 else ())
        if l == 0:
            w_up_bf = cast[0]
        x = _ffn(x, l, row(ffn_norm_g), w_up_bf, ffn_dw_w, w_dn_bf, final_g, l == depth - 1)
    return x
```

```python
import functools

import jax
import jax.numpy as jnp
from jax import lax
from jax.experimental import pallas as pl
from jax.experimental.pallas import tpu as pltpu

EPS = 1e-6
POOL_WINDOWS = (2, 4, 8, 16)
XA_HEADS = 4

F32 = jnp.float32
BF16 = jnp.bfloat16
U32 = jnp.uint32

LANES = 128
SUBLANES = 8
BF16_ROWS = 16
SEQ_TILE = 1024
FFN_SEQ_TILE = 512
NORM_ROWS = 256
CONV_ROWS = 128
KV_BATCH = 2
VMEM_LIMIT = 52 * 1024 * 1024


def _dot(a, b):
    return jnp.dot(a, b, preferred_element_type=F32)


def _rms(x, g):
    return x * lax.rsqrt(jnp.mean(x * x, axis=-1, keepdims=True) + EPS) * g


def _const_spec(shape, layer=None):
    n = len(shape)
    if layer is None:
        return pl.BlockSpec(shape, lambda b, s: (0,) * n, pipeline_mode=pl.Buffered(1))
    return pl.BlockSpec((None,) + shape, lambda b, s: (layer,) + (0,) * n,
                        pipeline_mode=pl.Buffered(1))


def _params():
    return pltpu.CompilerParams(dimension_semantics=("arbitrary", "arbitrary"),
                                vmem_limit_bytes=VMEM_LIMIT)


def _kv_kernel(mem_ref, g_ref, wkv_ref, *refs):
    n = (len(refs) - 3) // 2
    w_f32, (kt_ref, v_ref), w_bf, wkv_bf = refs[:n], refs[n:n + 2], refs[n + 2:2 * n + 2], refs[-1]
    nb, m, d = mem_ref.shape

    @pl.when(pl.program_id(1) == 0)
    def _():
        wkv_bf[...] = wkv_ref[...].astype(BF16)

    mn = _rms(mem_ref[...].reshape(nb * m, d), g_ref[...]).astype(BF16)
    k = _dot(mn, wkv_bf[:, :d])
    v_ref[...] = _dot(mn, wkv_bf[:, d:]).astype(BF16).reshape(nb, m, d)
    for i in range(nb):
        kt_ref[i] = k[i * m:(i + 1) * m, :].T.astype(BF16)
    for src, dst in zip(w_f32, w_bf):
        dst[...] = src[...].astype(BF16)


def _memory_kv_and_weight_casts(mem, mem_norm_g, w_kv, weights):
    depth = w_kv.shape[0]
    b, m, d = mem.shape
    nb = KV_BATCH if b % KV_BATCH == 0 else 1
    steps = b // nb
    slabs = [(w.shape[1] // steps, w.shape[2]) for w in weights]
    assert all(w.shape[1] % steps == 0 and r % BF16_ROWS == 0 for w, (r, _) in zip(weights, slabs))
    slab_specs = [pl.BlockSpec((None, r, c), lambda l, i: (l, i, 0)) for r, c in slabs]
    outs = pl.pallas_call(
        _kv_kernel,
        grid=(depth, steps),
        in_specs=[pl.BlockSpec((nb, m, d), lambda l, i: (i, 0, 0)),
                  pl.BlockSpec((1, d), lambda l, i: (0, 0)),
                  pl.BlockSpec((None, d, 2 * d), lambda l, i: (l, 0, 0),
                               pipeline_mode=pl.Buffered(1))] + slab_specs,
        out_specs=[pl.BlockSpec((None, nb, d, m), lambda l, i: (l, i, 0, 0)),
                   pl.BlockSpec((None, nb, m, d), lambda l, i: (l, i, 0, 0))] + slab_specs,
        out_shape=[jax.ShapeDtypeStruct((depth, b, d, m), BF16),
                   jax.ShapeDtypeStruct((depth, b, m, d), BF16)]
                  + [jax.ShapeDtypeStruct(w.shape, BF16) for w in weights],
        scratch_shapes=[pltpu.VMEM((d, 2 * d), BF16)],
        compiler_params=_params(),
        name="memory_kv",
    )(mem, mem_norm_g.reshape(1, d), w_kv, *weights)
    return outs[0], outs[1], outs[2:]


def _mixer_kernel(x_ref, g_ref, win_ref, dww_ref, dwb_ref, lng_ref, lnb_ref, wco_ref, wpg_ref,
                  ps_ref, wout_ref, o_ref, ybuf, ebuf, obuf, wbuf, tbuf, ubuf, cbuf, mbuf, hbuf,
                  *, hist):
    ts, d = x_ref.shape
    ncb = ybuf.shape[0]
    cc = ncb * LANES
    rows = hist + ts
    ngrp, gdim, gout = wpg_ref.shape
    cp = ngrp * gdim
    kc = dww_ref.shape[0]
    s = pl.program_id(1)

    @pl.when(s == 0)
    def _():
        ybuf[:, 0:hist, :] = jnp.zeros((ncb, hist, LANES), F32)
        ybuf[:, rows:rows + BF16_ROWS, :] = jnp.zeros((ncb, BF16_ROWS, LANES), F32)
        ubuf[:, 0:hist, :] = jnp.zeros((ngrp, hist, LANES), F32)
        for k in range(kc):
            wbuf[k] = jnp.broadcast_to(dww_ref[k:k + 1, :], (BF16_ROWS, cc)).astype(BF16)

    for r0 in range(0, ts, NORM_ROWS):
        rb = slice(r0, r0 + NORM_ROWS)
        hb = _rms(x_ref[rb, :], g_ref[...]).astype(BF16)
        hbuf[rb, :] = hb
        yc = _dot(hb, win_ref[:, 0:cc]) * jax.nn.sigmoid(_dot(hb, win_ref[:, cc:2 * cc]))
        for j in range(ncb):
            ybuf[j, hist + r0:hist + r0 + NORM_ROWS, :] = yc[:, j * LANES:(j + 1) * LANES]
    h = hbuf[...]
    u = _dot(h, win_ref[:, 2 * cc:2 * cc + cp])
    for j in range(ngrp):
        ubuf[j, hist:rows, :] = u[:, j * LANES:(j + 1) * LANES]

    for j in range(ncb):
        ebuf[j] = pltpu.bitcast(ybuf[j, 0:rows, :].astype(BF16), U32)
        obuf[j] = pltpu.bitcast(ybuf[j, 1:rows + 1, :].astype(BF16), U32)

    for j in range(ncb):
        lanes = slice(j * LANES, (j + 1) * LANES)
        taps = [jnp.concatenate([wbuf[k, :, lanes]] * (CONV_ROWS // BF16_ROWS), axis=0)
                for k in range(kc)]

        def conv_rows(n, carry, j=j, lanes=lanes, taps=taps):
            r0 = pl.multiple_of(n * CONV_ROWS, CONV_ROWS)
            w0 = pl.multiple_of(n * (CONV_ROWS // 2), CONV_ROWS // 2)
            acc = jnp.zeros((CONV_ROWS, LANES), F32)
            for k in range(kc):
                lo = hist - (kc - 1) + k
                src = ebuf if lo % 2 == 0 else obuf
                win = pltpu.bitcast(src[j, pl.ds(w0 + lo // 2, CONV_ROWS // 2), :], BF16)
                acc = acc + win.astype(F32) * taps[k].astype(F32)
            tbuf[pl.ds(r0, CONV_ROWS), lanes] = acc
            return carry

        lax.fori_loop(0, ts // CONV_ROWS, conv_rows, 0)

    yv = tbuf[...] + dwb_ref[...]
    xc = yv - jnp.mean(yv, axis=-1, keepdims=True)
    var = jnp.mean(xc * xc, axis=-1, keepdims=True)
    yn = xc * lax.rsqrt(var + EPS) * lng_ref[...] + lnb_ref[...]
    cbuf[...] = (yn * jax.nn.sigmoid(yn)).astype(BF16)

    t_glob = s * ts + lax.broadcasted_iota(jnp.int32, (ts, LANES), 0)
    g_conv0 = 2 * cc + cp
    g_pool0 = g_conv0 + d
    for j in range(ngrp):
        w = POOL_WINDOWS[j]
        cols = slice(j * gout, (j + 1) * gout)
        cur = ubuf[j, hist:rows, :]
        tot = cur
        for back in range(1, w):
            tot = tot + ubuf[j, hist - back:rows - back, :]
        cnt = jnp.minimum(t_glob + 1, w).astype(F32)
        zp = (tot / cnt - cur).astype(BF16)
        yp = _dot(zp, wpg_ref[j]) * ps_ref[:, cols]
        yo = _dot(cbuf[...], wco_ref[:, cols])
        gc = _dot(h, win_ref[:, g_conv0 + j * gout:g_conv0 + (j + 1) * gout])
        gp = _dot(h, win_ref[:, g_pool0 + j * gout:g_pool0 + (j + 1) * gout])
        mbuf[:, cols] = (jax.nn.sigmoid(gc) * yo + jax.nn.sigmoid(gp) * yp).astype(BF16)

    o_ref[...] = x_ref[...] + _dot(mbuf[...], wout_ref[...])

    ybuf[:, 0:hist, :] = ybuf[:, ts:rows, :]
    ubuf[:, 0:hist, :] = ubuf[:, ts:rows, :]


def _mixer(x, layer, g, w_in, dw_w, dw_b, ln_g, ln_b, w_co, w_pg, p_scale, w_out):
    b, s, d = x.shape
    ts = min(SEQ_TILE, s)
    n_in = w_in.shape[-1]
    kc, cc = dw_w.shape[1:]
    ngrp, gdim, gout = w_pg.shape[1:]
    cp = ngrp * gdim
    hist = -(-max(kc - 1, max(POOL_WINDOWS) - 1) // BF16_ROWS) * BF16_ROWS
    assert s % ts == 0 and ts % CONV_ROWS == 0 and CONV_ROWS % BF16_ROWS == 0 and hist <= ts
    assert ts % NORM_ROWS == 0
    assert cc % LANES == 0 and gdim == LANES
    assert n_in == 2 * cc + cp + 2 * d and ngrp == len(POOL_WINDOWS) and ngrp * gout == d
    xspec = pl.BlockSpec((None, ts, d), lambda i, j: (i, j, 0))
    return pl.pallas_call(
        functools.partial(_mixer_kernel, hist=hist),
        grid=(b, s // ts),
        in_specs=[xspec,
                  _const_spec((1, d), layer),
                  _const_spec((d, n_in), layer),
                  _const_spec((kc, cc), layer),
                  _const_spec((1, cc), layer),
                  _const_spec((1, cc), layer),
                  _const_spec((1, cc), layer),
                  _const_spec((cc, d), layer),
                  _const_spec((ngrp, gdim, gout), layer),
                  _const_spec((1, d), layer),
                  _const_spec((d, d), layer)],
        out_specs=xspec,
        out_shape=jax.ShapeDtypeStruct(x.shape, x.dtype),
        scratch_shapes=[pltpu.VMEM((cc // LANES, hist + ts + BF16_ROWS, LANES), F32),
                        pltpu.VMEM((cc // LANES, (hist + ts) // 2, LANES), U32),
                        pltpu.VMEM((cc // LANES, (hist + ts) // 2, LANES), U32),
                        pltpu.VMEM((kc, BF16_ROWS, cc), BF16),
                        pltpu.VMEM((ts, cc), F32),
                        pltpu.VMEM((ngrp, hist + ts, LANES), F32),
                        pltpu.VMEM((ts, cc), BF16),
                        pltpu.VMEM((ts, d), BF16),
                        pltpu.VMEM((ts, d), BF16)],
        compiler_params=_params(),
        name=f"mixer_l{layer}",
    )(x, g, w_in, dw_w, dw_b, ln_g, ln_b, w_co, w_pg, p_scale, w_out)


def _xattn_kernel(x_ref, g_ref, wq_ref, kt_ref, v_ref, wo_ref, *refs):
    n = (len(refs) - 3) // 2
    w_f32, o_ref, w_bf, (abuf, qbuf) = refs[:n], refs[n], refs[n + 1:2 * n + 1], refs[-2:]
    for src, dst in zip(w_f32, w_bf):
        dst[...] = src[...].astype(BF16)
    ts, d = x_ref.shape
    hd = d // XA_HEADS
    for r0 in range(0, ts, NORM_ROWS):
        rows = slice(r0, r0 + NORM_ROWS)
        hq = _rms(x_ref[rows, :], g_ref[...]).astype(BF16)
        qbuf[rows, :] = (_dot(hq, wq_ref[...]) * (hd ** -0.5)).astype(BF16)
    heads = [slice(hh * hd, (hh + 1) * hd) for hh in range(XA_HEADS)]
    scores = [_dot(qbuf[:, cols], kt_ref[cols, :]) for cols in heads]
    for cols, sc in zip(heads, scores):
        p = jnp.exp(sc - jnp.max(sc, axis=-1, keepdims=True))
        l = jnp.sum(p, axis=-1, keepdims=True)
        abuf[:, cols] = (_dot(p.astype(BF16), v_ref[:, cols]) / l).astype(BF16)
    o_ref[...] = x_ref[...] + _dot(abuf[...], wo_ref[...])


def _xattn(x, layer, g, w_q, kt, v, w_o, cast_weights=()):
    b, s, d = x.shape
    m = v.shape[2]
    ts = min(SEQ_TILE, s)
    ns = s // ts
    assert s % ts == 0 and ts % NORM_ROWS == 0 and d % XA_HEADS == 0
    def steps_per_slab(rows):
        return next(k for k in (1, 2, 4, 8) if rows * k % (b * ns) == 0
                    and rows * k // (b * ns) % BF16_ROWS == 0)
    span = [steps_per_slab(w.shape[1]) for w in cast_weights]
    slabs = [(w.shape[0], w.shape[1] * k // (b * ns), w.shape[2]) for w, k in zip(cast_weights, span)]
    slab_specs = [pl.BlockSpec(blk, lambda i, j, k=k: (0, (i * ns + j) // k, 0))
                  for blk, k in zip(slabs, span)]
    xspec = pl.BlockSpec((None, ts, d), lambda i, j: (i, j, 0))
    outs = pl.pallas_call(
        _xattn_kernel,
        grid=(b, ns),
        in_specs=[xspec, _const_spec((1, d), layer), _const_spec((d, d), layer),
                  pl.BlockSpec((None, None, d, m), lambda i, j: (layer, i, 0, 0)),
                  pl.BlockSpec((None, None, m, d), lambda i, j: (layer, i, 0, 0)),
                  _const_spec((d, d), layer)] + slab_specs,
        out_specs=[xspec] + slab_specs,
        out_shape=[jax.ShapeDtypeStruct(x.shape, x.dtype)]
                  + [jax.ShapeDtypeStruct(w.shape, BF16) for w in cast_weights],
        scratch_shapes=[pltpu.VMEM((ts, d), BF16), pltpu.VMEM((ts, d), BF16)],
        compiler_params=_params(),
        name=f"xattn_l{layer}",
    )(x, g, w_q, kt, v, w_o, *cast_weights)
    return outs[0], outs[1:]


def _gelu_tanh(x):
    return 0.5 * x * (1.0 + jnp.tanh(0.7978845608028654 * (x + 0.044715 * (x * x * x))))


def _ffn_kernel(x_ref, g_ref, wup_ref, dw_ref, wdn_ref, fg_ref, o_ref, hbuf, ebuf, hist_ref,
                *, chunk, final_norm):
    ts, d = x_ref.shape
    dff = wdn_ref.shape[0]
    kf = dw_ref.shape[0]
    pad = SUBLANES
    s = pl.program_id(1)

    @pl.when(s == 0)
    def _():
        hist_ref[...] = jnp.zeros(hist_ref.shape, F32)

    x = x_ref[...]
    hf = _rms(x, g_ref[...]).astype(BF16)

    def conv_half(c0):
        ebuf[0:pad, :] = hist_ref[:, c0:c0 + chunk]
        ebuf[pad:pad + ts, :] = _dot(hf, wup_ref[:, c0:c0 + chunk])
        hist_ref[:, c0:c0 + chunk] = ebuf[ts:ts + pad, :]
        out = dw_ref[kf - 1:kf, c0:c0 + chunk] * ebuf[pad:pad + ts, :]
        for back in range(1, kf):
            out = out + dw_ref[kf - 1 - back:kf - back, c0:c0 + chunk] * ebuf[pad - back:pad - back + ts, :]
        return out

    for c in range(dff // chunk):
        gate = conv_half(c * chunk)
        val = conv_half(dff + c * chunk)
        hbuf[:, c * chunk:(c + 1) * chunk] = (_gelu_tanh(gate) * val).astype(BF16)

    y = x + _dot(hbuf[...], wdn_ref[...])
    if final_norm:
        y = _rms(y, fg_ref[...])
    o_ref[...] = y


def _ffn(x, layer, g, w_up, dw_w, w_down, final_g, final_norm):
    b, s, d = x.shape
    dff = w_down.shape[1]
    kf = dw_w.shape[1]
    ts = min(FFN_SEQ_TILE, s)
    chunk = 2 * LANES
    assert s % ts == 0 and dff % chunk == 0 and kf - 1 <= SUBLANES
    xspec = pl.BlockSpec((None, ts, d), lambda i, j: (i, j, 0))
    return pl.pallas_call(
        functools.partial(_ffn_kernel, chunk=chunk, final_norm=final_norm),
        grid=(b, s // ts),
        in_specs=[xspec, _const_spec((1, d), layer), _const_spec((d, 2 * dff), layer),
                  _const_spec((kf, 2 * dff), layer), _const_spec((dff, d), layer),
                  _const_spec((1, d))],
        out_specs=xspec,
        out_shape=jax.ShapeDtypeStruct(x.shape, x.dtype),
        scratch_shapes=[pltpu.VMEM((ts, dff), BF16),
                        pltpu.VMEM((SUBLANES + ts, chunk), F32),
                        pltpu.VMEM((SUBLANES, 2 * dff), F32)],
        compiler_params=_params(),
        name=f"ffn_l{layer}",
    )(x, g, w_up, dw_w, w_down, final_g)


def kernel(x, mem, mix_norm_g, w_in, conv_dw_w, conv_dw_b, conv_ln_g, conv_ln_b, w_conv_out,
           w_pool_grp, pool_scale, w_out, xattn_norm_g, mem_norm_g, w_q, w_kv, w_o,
           ffn_norm_g, w_up, ffn_dw_w, w_down, final_norm_g):
    depth = w_in.shape[0]
    d = x.shape[-1]
    row = lambda v: v.reshape(v.shape[0], 1, v.shape[1])
    pg_shape = w_pool_grp.shape
    w_pg_rows = w_pool_grp.reshape(depth, pg_shape[1] * pg_shape[2], pg_shape[3])
    kt_all, v_all, (w_in_bf, w_co_bf, w_pg_bf, w_out_bf, w_q_bf, w_o_bf) = (
        _memory_kv_and_weight_casts(mem, mem_norm_g, w_kv,
                                    (w_in, w_conv_out, w_pg_rows, w_out, w_q, w_o)))
    w_pg_bf = w_pg_bf.reshape(pg_shape)
    final_g = final_norm_g.reshape(1, d)
    for l in range(depth):
        x = _mixer(x, l, row(mix_norm_g), w_in_bf, conv_dw_w, row(conv_dw_b), row(conv_ln_g),
                   row(conv_ln_b), w_co_bf, w_pg_bf, row(pool_scale), w_out_bf)
        x, cast = _xattn(x, l, row(xattn_norm_g), w_q_bf, kt_all, v_all, w_o_bf,
                         (w_up, w_down) if l == 0 else ())
        if l == 0:
            w_up_bf, w_dn_bf = cast
        x = _ffn(x, l, row(ffn_norm_g), w_up_bf, ffn_dw_w, w_dn_bf, final_g, l == depth - 1)
    return x
```
